```python
import math
import jax, jax.numpy as jnp
from jax import lax
import numpy as np


D_MODEL = 4096
BATCH = 2
SEQ = 8192
DEPTH = 1

GRID_W = 64
CTX_LEN = 256
MIX_WIDTH = D_MODEL
LRU_WIDTH = MIX_WIDTH // 2
LRU_BLOCKS = 16
LRU_BLOCK_DIM = LRU_WIDTH // LRU_BLOCKS
LRU_C = 8.0
GDN_WIDTH = MIX_WIDTH - LRU_WIDTH
GDN_HEADS = 16
GDN_HEAD_DIM = GDN_WIDTH // GDN_HEADS
GDN_CHUNK = 64
CONV_W = 4
CONV_PAD_L = 2
CONV_PAD_R = 1
N_DIR = 2
D_FF = -(-(8 * D_MODEL) // (3 * 256)) * 256
N_MOD = 6
EPS = 1e-6
COL_LRU_X = 0
COL_LRU_GATE = COL_LRU_X + LRU_WIDTH
COL_QKV = COL_LRU_GATE + LRU_WIDTH
COL_Z = COL_QKV + 3 * GDN_WIDTH
COL_BETA = COL_Z + GDN_WIDTH
COL_ALPHA = COL_BETA + N_DIR * GDN_HEADS
N_IN = COL_ALPHA + N_DIR * GDN_HEADS

kernel_name = 'hybrid_rglru_gdn_dit_layer'


def _rmsnorm(x, g):
    xf = x.astype(jnp.float32)
    y = xf * lax.rsqrt(jnp.mean(xf * xf, axis=-1, keepdims=True) + EPS)
    return (y * g.astype(jnp.float32)).astype(x.dtype)


def _l2norm(x):
    xf = x.astype(jnp.float32)
    return xf * lax.rsqrt(jnp.sum(xf * xf, axis=-1, keepdims=True) + EPS)


def _modulate(h, shift, scale):
    return h * (1.0 + scale) + shift


def _cols(p, start, width):
    return p[..., start:start + width]


def _dwconv_centred(x, w):
    t = x.shape[1]
    xp = jnp.pad(x, ((0, 0), (CONV_PAD_L, CONV_PAD_R), (0, 0)))
    y = xp[:, 0:t] * w[0]
    for j in range(1, CONV_W):
        y = y + xp[:, j:j + t] * w[j]
    return y


def _raster_to_colmajor(t, rows):
    b, n = t.shape[:2]
    rest = t.shape[2:]
    return t.reshape((b, rows, GRID_W) + rest).swapaxes(1, 2).reshape((b, n) + rest)


def _colmajor_to_raster(t, rows):
    b, n = t.shape[:2]
    rest = t.shape[2:]
    return t.reshape((b, GRID_W, rows) + rest).swapaxes(1, 2).reshape((b, n) + rest)


def _flip(t, rev):
    return t[:, ::-1] if rev else t


def _lin_combine(e1, e2):
    a1, b1 = e1
    a2, b2 = e2
    return a1 * a2, a2 * b1 + b2


def _linear_scan(a, b, h0, rev):
    first = -1 if rev else 0
    if h0 is not None:
        b = b.at[:, first].add(a[:, first] * h0)
    _, h = lax.associative_scan(_lin_combine, (a, b), reverse=rev, axis=1)
    final = h[:, 0] if rev else h[:, -1]
    return h, final


def _rglru_coeffs(xc, w_a, b_a, w_x, b_x, lam):
    bsz, n, _ = xc.shape
    xb = xc.reshape(bsz, n, LRU_BLOCKS, LRU_BLOCK_DIM)
    r = jax.nn.sigmoid(jnp.einsum('btni,nij->btnj', xb, w_a).reshape(bsz, n, LRU_WIDTH) + b_a)
    i = jax.nn.sigmoid(jnp.einsum('btni,nij->btnj', xb, w_x).reshape(bsz, n, LRU_WIDTH) + b_x)
    log_a = -LRU_C * r * jax.nn.softplus(-lam)
    a = jnp.exp(log_a)
    return a, jnp.sqrt(-jnp.expm1(2.0 * log_a)) * (i * xc)


def _rglru_mixer(u_lat, gate_lat, u_ctx, gate_ctx, conv_w, conv_b, w_a, b_a, w_x, b_x, lam, norm_g, with_ctx):
    xl = (_dwconv_centred(u_lat, conv_w) + conv_b).astype(jnp.float32)
    xc = (_dwconv_centred(u_ctx, conv_w) + conv_b).astype(jnp.float32)
    h_lat, h_ctx = [], []
    for d in range(N_DIR):
        rev = d == 1
        a_c, b_c = _rglru_coeffs(xc, w_a[d], b_a[d], w_x[d], b_x[d], lam[d])
        hc, hc_final = _linear_scan(a_c, b_c, None, rev)
        a_l, b_l = _rglru_coeffs(xl, w_a[d], b_a[d], w_x[d], b_x[d], lam[d])
        hl, _ = _linear_scan(a_l, b_l, hc_final, rev)
        h_lat.append(hl)
        h_ctx.append(hc)
    y_lat = _rmsnorm(((h_lat[0] + h_lat[1]) * jax.nn.gelu(gate_lat.astype(jnp.float32))).astype(u_lat.dtype), norm_g)
    y_ctx = None
    if with_ctx:
        y_ctx = _rmsnorm(((h_ctx[0] + h_ctx[1]) * jax.nn.gelu(gate_ctx.astype(jnp.float32))).astype(u_ctx.dtype), norm_g)
    return y_lat, y_ctx


def _to_chunks(t):
    b, n, h = t.shape[:3]
    rest = t.shape[3:]
    t = t.reshape((b, n // GDN_CHUNK, GDN_CHUNK, h) + rest)
    return jnp.moveaxis(t, (1, 3), (0, 2))


def _gdn_chunked(q, k, v, g, beta, s0, need_out):
    bsz, n, h, dk = q.shape
    dv = v.shape[-1]
    q, k, v, g, beta = (_to_chunks(t.astype(jnp.float32)) for t in (q, k, v, g, beta))
    q = q * (dk ** -0.5)
    gcum = jnp.cumsum(g, axis=-1)
    idx = jnp.arange(GDN_CHUNK)
    lower = idx[:, None] >= idx[None, :]
    strict = idx[:, None] > idx[None, :]
    diff = gcum[..., :, None] - gcum[..., None, :]
    decay = jnp.where(lower, jnp.exp(jnp.where(lower, diff, 0.0)), 0.0)
    kb = k * beta[..., None]
    lmat = jnp.where(strict, jnp.einsum('nbhid,nbhjd->nbhij', kb, k) * decay, 0.0)
    amat = lmat + jnp.eye(GDN_CHUNK, dtype=jnp.float32)
    rhs = jnp.concatenate([v * beta[..., None], kb * jnp.exp(gcum)[..., None]], axis=-1)
    sol = lax.linalg.triangular_solve(amat, rhs, left_side=True, lower=True, unit_diagonal=True)
    u, w = sol[..., :dv], sol[..., dv:]
    k_tail = k * jnp.exp(gcum[..., -1:] - gcum)[..., None]
    g_last = jnp.exp(gcum[..., -1])
    if s0 is None:
        s0 = jnp.zeros((bsz, h, dk, dv), jnp.float32)
    if need_out:
        intra = jnp.where(lower, jnp.einsum('nbhid,nbhjd->nbhij', q, k) * decay, 0.0)
        xs = (w, u, k_tail, g_last, q * jnp.exp(gcum)[..., None], intra)
    else:
        xs = (w, u, k_tail, g_last)

    def step(s, xs_i):
        w_i, u_i, kt_i, gl_i = xs_i[:4]
        v_new = u_i - jnp.einsum('bhck,bhkv->bhcv', w_i, s)
        s_new = s * gl_i[..., None, None] + jnp.einsum('bhck,bhcv->bhkv', kt_i, v_new)
        if not need_out:
            return s_new, None
        qd_i, intra_i = xs_i[4:]
        o_i = jnp.einsum('bhck,bhkv->bhcv', qd_i, s) + jnp.einsum('bhij,bhjv->bhiv', intra_i, v_new)
        return s_new, o_i

    s_final, o = lax.scan(step, s0.astype(jnp.float32), xs)
    if need_out:
        o = jnp.moveaxis(o, (0, 2), (1, 3)).reshape(bsz, n, h, dv)
    return o, s_final


def _gdn_qkv(qkv, conv_w):
    bsz, n, _ = qkv.shape
    y = jax.nn.silu(_dwconv_centred(qkv, conv_w))
    q, k, v = jnp.split(y, 3, axis=-1)
    shp = (bsz, n, GDN_HEADS, GDN_HEAD_DIM)
    return _l2norm(q.reshape(shp)), _l2norm(k.reshape(shp)), v.reshape(shp)


def _gdn_gates(p_beta, p_alpha, a_log_d, dt_bias_d):
    beta = jax.nn.sigmoid(p_beta.astype(jnp.float32))
    g = -jnp.exp(a_log_d.astype(jnp.float32)) * jax.nn.softplus(p_alpha.astype(jnp.float32) + dt_bias_d)
    return beta, g


def _gated_headnorm(o, z, g):
    bsz, n = z.shape[:2]
    zf = z.reshape(bsz, n, GDN_HEADS, GDN_HEAD_DIM).astype(jnp.float32)
    y = _rmsnorm(o, g) * jax.nn.silu(zf)
    return y.reshape(bsz, n, GDN_WIDTH).astype(z.dtype)


def _gdn_mixer(qkv_lat, beta_lat, alpha_lat, z_lat, qkv_ctx, beta_ctx, alpha_ctx, z_ctx,
               conv_w, a_log, dt_bias, norm_g, rows, with_ctx):
    qkv_lat, beta_lat, alpha_lat = (_raster_to_colmajor(t, rows) for t in (qkv_lat, beta_lat, alpha_lat))
    lat = _gdn_qkv(qkv_lat, conv_w)
    cqkv = _gdn_qkv(qkv_ctx, conv_w)
    o_lat, o_ctx = [], []
    for d in range(N_DIR):
        rev = d == 1
        hs = slice(d * GDN_HEADS, (d + 1) * GDN_HEADS)
        b_c, g_c = _gdn_gates(beta_ctx[..., hs], alpha_ctx[..., hs], a_log[d], dt_bias[d])
        b_l, g_l = _gdn_gates(beta_lat[..., hs], alpha_lat[..., hs], a_log[d], dt_bias[d])
        oc, s_ctx = _gdn_chunked(*(_flip(t, rev) for t in cqkv + (g_c, b_c)), None, with_ctx)
        ol, _ = _gdn_chunked(*(_flip(t, rev) for t in lat + (g_l, b_l)), s_ctx, True)
        o_lat.append(_flip(ol, rev))
        if with_ctx:
            o_ctx.append(_flip(oc, rev))
    y_lat = _gated_headnorm(_colmajor_to_raster(o_lat[0] + o_lat[1], rows), z_lat, norm_g)
    y_ctx = _gated_headnorm(o_ctx[0] + o_ctx[1], z_ctx, norm_g) if with_ctx else None
    return y_lat, y_ctx


def _swiglu(h, wg, wu, wd):
    return (jax.nn.silu(h @ wg) * (h @ wu)) @ wd


def setup_inputs(seed: int = 0) -> dict:
    key = jax.random.key(seed)
    ks = jax.random.split(key, 32)
    L = DEPTH
    f32 = jnp.float32

    def nrm(k, shape, scale):
        return jax.random.normal(k, shape, f32) * scale

    def gain(k, shape):
        return 1.0 + 0.05 * jax.random.normal(k, shape, f32)

    u = jax.random.uniform(ks[17], (L, N_DIR, LRU_WIDTH), f32, minval=0.9, maxval=0.999)
    s = u ** (1.0 / LRU_C)
    lru_lambda = jnp.log(s) - jnp.log1p(-s)
    gdn_a_log = jnp.log(jax.random.uniform(ks[20], (L, N_DIR, GDN_HEADS), f32, minval=1.0, maxval=16.0))
    dt = jnp.exp(jax.random.uniform(ks[21], (L, N_DIR, GDN_HEADS), f32,
                                    minval=math.log(1e-3), maxval=math.log(1e-1)))
    gdn_dt_bias = dt + jnp.log(-jnp.expm1(-dt))
    return {
        'x': nrm(ks[0], (BATCH, SEQ, D_MODEL), 1.0),
        'c': nrm(ks[1], (BATCH, D_MODEL), 1.0),
        'ctx': nrm(ks[2], (BATCH, CTX_LEN, D_MODEL), 1.0),
        'c_ctx': nrm(ks[3], (D_MODEL,), 1.0),
        'w_ada': nrm(ks[4], (L, D_MODEL, N_MOD * D_MODEL), 0.5 * D_MODEL ** -0.5),
        'b_ada': nrm(ks[5], (L, N_MOD * D_MODEL), 0.01),
        'g_pre_mix': gain(ks[6], (L, D_MODEL)),
        'g_post_mix': gain(ks[7], (L, D_MODEL)),
        'g_pre_ffn': gain(ks[8], (L, D_MODEL)),
        'g_post_ffn': gain(ks[9], (L, D_MODEL)),
        'w_in': nrm(ks[10], (L, D_MODEL, N_IN), D_MODEL ** -0.5),
        'lru_conv_w': nrm(ks[11], (L, CONV_W, LRU_WIDTH), CONV_W ** -0.5),
        'lru_conv_b': nrm(ks[12], (L, LRU_WIDTH), 0.01),
        'lru_w_a': nrm(ks[13], (L, N_DIR, LRU_BLOCKS, LRU_BLOCK_DIM, LRU_BLOCK_DIM), LRU_BLOCK_DIM ** -0.5),
        'lru_b_a': nrm(ks[14], (L, N_DIR, LRU_WIDTH), 0.01),
        'lru_w_x': nrm(ks[15], (L, N_DIR, LRU_BLOCKS, LRU_BLOCK_DIM, LRU_BLOCK_DIM), LRU_BLOCK_DIM ** -0.5),
        'lru_b_x': nrm(ks[16], (L, N_DIR, LRU_WIDTH), 0.01),
        'lru_lambda': lru_lambda,
        'lru_norm_g': gain(ks[18], (L, LRU_WIDTH)),
        'gdn_conv_w': nrm(ks[19], (L, CONV_W, 3 * GDN_WIDTH), CONV_W ** -0.5),
        'gdn_a_log': gdn_a_log,
        'gdn_dt_bias': gdn_dt_bias,
        'gdn_norm_g': gain(ks[22], (L, GDN_HEAD_DIM)),
        'w_out': nrm(ks[23], (L, MIX_WIDTH, D_MODEL), MIX_WIDTH ** -0.5),
        'w_ffn_gate': nrm(ks[24], (L, D_MODEL, D_FF), D_MODEL ** -0.5),
        'w_ffn_up': nrm(ks[25], (L, D_MODEL, D_FF), D_MODEL ** -0.5),
        'w_ffn_down': nrm(ks[26], (L, D_FF, D_MODEL), D_FF ** -0.5),
    }


def reference(x, c, ctx, c_ctx, w_ada, b_ada, g_pre_mix, g_post_mix, g_pre_ffn, g_post_ffn, w_in,
              lru_conv_w, lru_conv_b, lru_w_a, lru_b_a, lru_w_x, lru_b_x, lru_lambda, lru_norm_g,
              gdn_conv_w, gdn_a_log, gdn_dt_bias, gdn_norm_g, w_out, w_ffn_gate, w_ffn_up, w_ffn_down):
    rows = x.shape[1] // GRID_W
    for l in range(DEPTH):
        with_ctx = l < DEPTH - 1
        shift_m, scale_m, gate_m, shift_f, scale_f, gate_f = jnp.split(
            (jax.nn.silu(c) @ w_ada[l] + b_ada[l])[:, None, :], N_MOD, axis=-1)
        mod_ctx = jnp.split((jax.nn.silu(c_ctx) @ w_ada[l] + b_ada[l])[None, None, :], N_MOD, axis=-1)

        h_lat = _modulate(_rmsnorm(x, g_pre_mix[l]), shift_m, scale_m)
        h_ctx = _modulate(_rmsnorm(ctx, g_pre_mix[l]), mod_ctx[0], mod_ctx[1])
        p_lat = h_lat @ w_in[l]
        p_ctx = h_ctx @ w_in[l]
        lru_lat, lru_ctx = _rglru_mixer(
            _cols(p_lat, COL_LRU_X, LRU_WIDTH), _cols(p_lat, COL_LRU_GATE, LRU_WIDTH),
            _cols(p_ctx, COL_LRU_X, LRU_WIDTH), _cols(p_ctx, COL_LRU_GATE, LRU_WIDTH),
            lru_conv_w[l], lru_conv_b[l], lru_w_a[l], lru_b_a[l], lru_w_x[l], lru_b_x[l],
            lru_lambda[l], lru_norm_g[l], with_ctx)
        gdn_lat, gdn_ctx = _gdn_mixer(
            _cols(p_lat, COL_QKV, 3 * GDN_WIDTH), _cols(p_lat, COL_BETA, N_DIR * GDN_HEADS),
            _cols(p_lat, COL_ALPHA, N_DIR * GDN_HEADS), _cols(p_lat, COL_Z, GDN_WIDTH),
            _cols(p_ctx, COL_QKV, 3 * GDN_WIDTH), _cols(p_ctx, COL_BETA, N_DIR * GDN_HEADS),
            _cols(p_ctx, COL_ALPHA, N_DIR * GDN_HEADS), _cols(p_ctx, COL_Z, GDN_WIDTH),
            gdn_conv_w[l], gdn_a_log[l], gdn_dt_bias[l], gdn_norm_g[l], rows, with_ctx)
        mix_lat = jnp.concatenate([lru_lat, gdn_lat], axis=-1)
        x = x + gate_m * _rmsnorm(mix_lat @ w_out[l], g_post_mix[l])

        h = _modulate(_rmsnorm(x, g_pre_ffn[l]), shift_f, scale_f)
        x = x + gate_f * _rmsnorm(_swiglu(h, w_ffn_gate[l], w_ffn_up[l], w_ffn_down[l]), g_post_ffn[l])

        if with_ctx:
            mix_ctx = jnp.concatenate([lru_ctx, gdn_ctx], axis=-1)
            ctx = ctx + mod_ctx[2] * _rmsnorm(mix_ctx @ w_out[l], g_post_mix[l])
            hc = _modulate(_rmsnorm(ctx, g_pre_ffn[l]), mod_ctx[3], mod_ctx[4])
            ctx = ctx + mod_ctx[5] * _rmsnorm(_swiglu(hc, w_ffn_gate[l], w_ffn_up[l], w_ffn_down[l]), g_post_ffn[l])
    return x
```

```python
import functools

import jax
import jax.numpy as jnp
from jax import lax
from jax.experimental import pallas as pl
from jax.experimental.pallas import tpu as pltpu

F32 = jnp.float32
BF16 = jnp.bfloat16

EPS = 1e-6
GRID_W = 64
CHUNK = 64
LRU_C = 8.0
CONV_W = 4
N_MOD = 6
LANES = 128
SUBLANES = 8
VMEM_LIMIT = 56 * 1024 * 1024


def _cparams(sem):
    return pltpu.CompilerParams(dimension_semantics=sem, vmem_limit_bytes=VMEM_LIMIT)


def _mm(a, b):
    return jnp.dot(a.astype(BF16), b.astype(BF16), preferred_element_type=F32)


def _mm_nt(a, b):
    return lax.dot_general(a.astype(BF16), b.astype(BF16), (((1,), (1,)), ((), ())),
                           preferred_element_type=F32)


def _mm_tn(a, b):
    return lax.dot_general(a.astype(BF16), b.astype(BF16), (((0,), (0,)), ((), ())),
                           preferred_element_type=F32)


def _sigmoid(x):
    return 1.0 / (1.0 + jnp.exp(-x))


def _silu(x):
    return x * _sigmoid(x)


def _softplus(x):
    return jnp.maximum(x, 0.0) + jnp.log1p(jnp.exp(-jnp.abs(x)))


def _tile(n, target, mult=LANES):
    if n <= target:
        return n
    t = (target // mult) * mult
    while t >= mult:
        if n % t == 0:
            return t
        t -= mult
    return n


def _ada_kernel(c_ref, w_ref, b_ref, o_ref):
    o_ref[...] = _mm(_silu(c_ref[...]), w_ref[...]) + b_ref[...]


def _ada(crows, w, b):
    r, d = crows.shape
    n = w.shape[1]
    tn = _tile(n, 512)
    return pl.pallas_call(
        _ada_kernel,
        out_shape=jax.ShapeDtypeStruct((r, n), F32),
        grid=(n // tn,),
        in_specs=[pl.BlockSpec((r, d), lambda j: (0, 0)),
                  pl.BlockSpec((d, tn), lambda j: (0, j)),
                  pl.BlockSpec((1, tn), lambda j: (0, j))],
        out_specs=pl.BlockSpec((r, tn), lambda j: (0, j)),
        compiler_params=_cparams(("arbitrary",)),
        name="ada",
    )(crows, w, b)


ROW_CHUNK = 64


def _norm_mod_rows(x, g, sc, sh):
    y = x * lax.rsqrt(jnp.mean(x * x, axis=-1, keepdims=True) + EPS) * g
    return y * (1.0 + sc) + sh


def _inproj_kernel(x_ref, g_ref, sh_ref, sc_ref, w_ref, wba_ref, o_ref, oba_ref, h_scr):
    tm = x_ref.shape[0]

    @pl.when(pl.program_id(2) == 0)
    def _():
        g = g_ref[...]
        sc = sc_ref[...]
        sh = sh_ref[...]

        def body(r, carry):
            rows = pl.ds(pl.multiple_of(r * ROW_CHUNK, ROW_CHUNK), ROW_CHUNK)
            h_scr[rows, :] = _norm_mod_rows(x_ref[rows, :], g, sc, sh).astype(BF16)
            return carry

        lax.fori_loop(0, tm // ROW_CHUNK, body, 0)
        oba_ref[...] = jnp.dot(h_scr[...], wba_ref[...], preferred_element_type=F32)

    o_ref[...] = jnp.dot(h_scr[...], w_ref[...], preferred_element_type=F32)


def _inproj(x, g, sh, sc, w_main, w_ba):
    b, t, d = x.shape
    n = w_main.shape[1]
    nba = w_ba.shape[1]
    tm = _tile(t, 512, SUBLANES)
    tn = _tile(n, 1024)
    per_batch = sh.shape[0] > 1
    mod_map = (lambda bb, i, j: (bb, 0, 0)) if per_batch else (lambda bb, i, j: (0, 0, 0))
    return pl.pallas_call(
        _inproj_kernel,
        out_shape=(jax.ShapeDtypeStruct((b, t, n), F32), jax.ShapeDtypeStruct((b, t, nba), F32)),
        grid=(b, t // tm, n // tn),
        in_specs=[pl.BlockSpec((None, tm, d), lambda bb, i, j: (bb, i, 0)),
                  pl.BlockSpec((1, d), lambda bb, i, j: (0, 0)),
                  pl.BlockSpec((None, 1, d), mod_map),
                  pl.BlockSpec((None, 1, d), mod_map),
                  pl.BlockSpec((d, tn), lambda bb, i, j: (0, j)),
                  pl.BlockSpec((d, nba), lambda bb, i, j: (0, 0))],
        out_specs=(pl.BlockSpec((None, tm, tn), lambda bb, i, j: (bb, i, j)),
                   pl.BlockSpec((None, tm, nba), lambda bb, i, j: (bb, i, 0))),
        scratch_shapes=[pltpu.VMEM((tm, d), BF16)],
        compiler_params=_cparams(("parallel", "parallel", "arbitrary")),
        name="inproj",
    )(x, g, sh, sc, w_main, w_ba)


def _fill_padded(xp_ref, x_ref, n, pad):
    zeros = jnp.zeros((pad, xp_ref.shape[1]), F32)
    xp_ref[0:pad, :] = zeros
    xp_ref[pad + n:pad + n + pad, :] = zeros
    step = min(n, 512)

    def body(i, carry):
        rows = pl.ds(pl.multiple_of(i * step, step), step)
        xp_ref[pl.ds(pl.multiple_of(pad + i * step, SUBLANES), step), :] = x_ref[rows, :]
        return carry

    lax.fori_loop(0, n // step, body, 0)


def _conv_rows(xp_ref, cw_ref, t0, rows, pad, stride):
    acc = None
    for j in range(CONV_W):
        off = pad + (j - 2) * stride
        term = cw_ref[j:j + 1, :] * xp_ref[pl.ds(t0 + off, rows), :]
        acc = term if acc is None else acc + term
    return acc


LRU_ROWS = 256
LRU_PAD = 8


def _lru_coeffs(xc, d, wa_ref, ba_ref, wx_ref, bx_ref, lam_ref):
    xb = xc.astype(BF16)
    r = _sigmoid(jnp.dot(xb, wa_ref[d], preferred_element_type=F32) + ba_ref[d:d + 1, :])
    i = _sigmoid(jnp.dot(xb, wx_ref[d], preferred_element_type=F32) + bx_ref[d:d + 1, :])
    log_a = (-LRU_C) * r * _softplus(-lam_ref[d:d + 1, :])
    a = jnp.exp(log_a)
    th = jnp.tanh(log_a)
    return a, jnp.sqrt(-2.0 * th / (1.0 - th)) * (i * xc)


def _lru_scan_chunk(a, b, h, rev):
    rows, c = a.shape
    nt = rows // SUBLANES
    a3 = a.reshape(nt, SUBLANES, c)
    b3 = b.reshape(nt, SUBLANES, c)
    ridx = lax.broadcasted_iota(jnp.int32, (nt, SUBLANES, c), 1)
    for s in (1, 2, 4):
        shift = (SUBLANES - s) if rev else s
        a_s = pltpu.roll(a3, shift, 1)
        b_s = pltpu.roll(b3, shift, 1)
        m = (ridx < SUBLANES - s) if rev else (ridx >= s)
        b3 = jnp.where(m, a3 * b_s + b3, b3)
        a3 = jnp.where(m, a3 * a_s, a3)
    out = [None] * nt
    order = range(nt - 1, -1, -1) if rev else range(nt)
    for t in order:
        ht = b3[t] + a3[t] * h
        out[t] = ht
        h = ht[0:1, :] if rev else ht[SUBLANES - 1:SUBLANES, :]
    return jnp.concatenate(out, axis=0), h


def _lru_kernel(ul_ref, gl_ref, uc_ref, cw_ref, cb_ref, wa_ref, ba_ref, wx_ref, bx_ref, lam_ref,
                o_ref, xpl_ref, xpc_ref):
    t_lat = ul_ref.shape[0]
    t_ctx = uc_ref.shape[0]
    c = ul_ref.shape[1]
    rl = min(LRU_ROWS, t_lat)
    rc = min(LRU_ROWS, t_ctx)
    _fill_padded(xpl_ref, ul_ref, t_lat, LRU_PAD)
    _fill_padded(xpc_ref, uc_ref, t_ctx, LRU_PAD)
    cb = cb_ref[...]

    for d in range(2):
        rev = d == 1
        prm = (wa_ref, ba_ref, wx_ref, bx_ref, lam_ref)

        def ctx_body(i, h, rev=rev, d=d, prm=prm):
            blk = (t_ctx // rc - 1 - i) if rev else i
            t0 = pl.multiple_of(blk * rc, rc)
            xc = _conv_rows(xpc_ref, cw_ref, t0, rc, LRU_PAD, 1) + cb
            a, b = _lru_coeffs(xc, d, *prm)
            _, h = _lru_scan_chunk(a, b, h, rev)
            return h

        h = lax.fori_loop(0, t_ctx // rc, ctx_body, jnp.zeros((1, c), F32))

        def lat_body(i, h, rev=rev, d=d, prm=prm):
            blk = (t_lat // rl - 1 - i) if rev else i
            t0 = pl.multiple_of(blk * rl, rl)
            rows = pl.ds(t0, rl)
            xc = _conv_rows(xpl_ref, cw_ref, t0, rl, LRU_PAD, 1) + cb
            a, b = _lru_coeffs(xc, d, *prm)
            hs, h = _lru_scan_chunk(a, b, h, rev)
            if rev:
                o_ref[rows, :] = (o_ref[rows, :] + hs) * jax.nn.gelu(gl_ref[rows, :], approximate=True)
            else:
                o_ref[rows, :] = hs
            return h

        lax.fori_loop(0, t_lat // rl, lat_body, h)


def _lru(p_lat, p_ctx, conv_w, conv_b, w_a, b_a, w_x, b_x, lam, lw):
    b, t, _ = p_lat.shape
    tc = p_ctx.shape[1]
    nb, bd = w_a.shape[1], w_a.shape[2]
    assert bd == LANES and nb * bd == lw
    vec = lambda bb, n: (0, n)
    return pl.pallas_call(
        _lru_kernel,
        out_shape=jax.ShapeDtypeStruct((b, t, lw), F32),
        grid=(b, nb),
        in_specs=[pl.BlockSpec((None, t, bd), lambda bb, n: (bb, 0, n)),
                  pl.BlockSpec((None, t, bd), lambda bb, n: (bb, 0, nb + n)),
                  pl.BlockSpec((None, tc, bd), lambda bb, n: (bb, 0, n)),
                  pl.BlockSpec((CONV_W, bd), vec),
                  pl.BlockSpec((1, bd), vec),
                  pl.BlockSpec((2, None, bd, bd), lambda bb, n: (0, n, 0, 0)),
                  pl.BlockSpec((2, bd), vec),
                  pl.BlockSpec((2, None, bd, bd), lambda bb, n: (0, n, 0, 0)),
                  pl.BlockSpec((2, bd), vec),
                  pl.BlockSpec((2, bd), vec)],
        out_specs=pl.BlockSpec((None, t, bd), lambda bb, n: (bb, 0, n)),
        scratch_shapes=[pltpu.VMEM((t + 2 * LRU_PAD, bd), F32),
                        pltpu.VMEM((tc + 2 * LRU_PAD, bd), F32)],
        compiler_params=_cparams(("parallel", "parallel")),
        name="lru",
    )(p_lat, p_lat, p_ctx, conv_w, conv_b, w_a, b_a, w_x, b_x, lam)


def _gdnpre_kernel(stride, n_heads, x_ref, cw_ref, o_ref, xp_ref, y_ref):
    t = x_ref.shape[0]
    pad = max(2 * stride, SUBLANES)
    _fill_padded(xp_ref, x_ref, t, pad)
    step = min(t, 512)

    def conv_body(i, carry):
        t0 = pl.multiple_of(i * step, step)
        y_ref[pl.ds(t0, step), :] = _conv_rows(xp_ref, cw_ref, t0, step, pad, stride)
        return carry

    lax.fori_loop(0, t // step, conv_body, 0)

    if stride > 1:
        w = stride
        last = t - w
        col = lax.broadcasted_iota(jnp.int32, (w, 1), 0)
        not_first = col >= 1
        not_last = col <= w - 2
        prev_bot = xp_ref[pl.ds(pad + last - 1, w), :]
        prev_bot2 = xp_ref[pl.ds(pad + last - w - 1, w), :]
        next_top = xp_ref[pl.ds(pad + 1, w), :]
        y_ref[0:w, :] = y_ref[0:w, :] + jnp.where(
            not_first, cw_ref[1:2, :] * prev_bot + cw_ref[0:1, :] * prev_bot2, 0.0)
        y_ref[w:2 * w, :] = y_ref[w:2 * w, :] + jnp.where(not_first, cw_ref[0:1, :] * prev_bot, 0.0)
        y_ref[last:t, :] = y_ref[last:t, :] + jnp.where(not_last, cw_ref[3:4, :] * next_top, 0.0)

    blk = pl.program_id(1)
    hd = x_ref.shape[1]

    def finish(normed, scale):
        def body(i, carry):
            rows = pl.ds(pl.multiple_of(i * step, step), step)
            y = _silu(y_ref[rows, :])
            if normed:
                y = y * (lax.rsqrt(jnp.sum(y * y, axis=-1, keepdims=True) + EPS) * scale)
            o_ref[rows, :] = y.astype(BF16)
            return carry

        lax.fori_loop(0, t // step, body, 0)

    @pl.when(blk < n_heads)
    def _():
        finish(True, hd ** -0.5)

    @pl.when(jnp.logical_and(blk >= n_heads, blk < 2 * n_heads))
    def _():
        finish(True, 1.0)

    @pl.when(blk >= 2 * n_heads)
    def _():
        finish(False, 1.0)


def _gdnpre(p, col0, conv_w, n_heads, stride):
    b, t, _ = p.shape
    hd = LANES
    nblk = 3 * n_heads
    assert col0 % hd == 0
    pad = max(2 * stride, SUBLANES)
    return pl.pallas_call(
        functools.partial(_gdnpre_kernel, stride, n_heads),
        out_shape=jax.ShapeDtypeStruct((b, t, nblk * hd), BF16),
        grid=(b, nblk),
        in_specs=[pl.BlockSpec((None, t, hd), lambda bb, n: (bb, 0, col0 // hd + n)),
                  pl.BlockSpec((CONV_W, hd), lambda bb, n: (0, n))],
        out_specs=pl.BlockSpec((None, t, hd), lambda bb, n: (bb, 0, n)),
        scratch_shapes=[pltpu.VMEM((t + 2 * pad, hd), F32), pltpu.VMEM((t, hd), F32)],
        compiler_params=_cparams(("parallel", "parallel")),
        name="gdnpre",
    )(p, conv_w)


GDN_HEAD_UNROLL = 4


def _split3(x):
    hi = x.astype(BF16)
    r1 = x - hi.astype(F32)
    mid = r1.astype(BF16)
    lo = (r1 - mid.astype(F32)).astype(BF16)
    return hi, mid, lo


def _gdn_kernel(mode, rev, d, gh, *refs):
    if mode == "ctx":
        q_ref, k_ref, v_ref, ba_ref, prm_ref, sfin_ref, s_scr, rowb_scr, colb_scr = refs
    elif mode == "fwd":
        q_ref, k_ref, v_ref, ba_ref, prm_ref, s0_ref, o_ref, s_scr, rowb_scr, colb_scr = refs
    else:
        (q_ref, k_ref, v_ref, ba_ref, prm_ref, s0_ref, of_ref, z_ref, ng_ref, o_ref,
         s_scr, rowb_scr, colb_scr) = refs
    r = q_ref.shape[0]
    hd = LANES
    nch = r // CHUNK
    step = pl.program_id(1)
    nsteps = pl.num_programs(1)

    @pl.when(step == 0)
    def _():
        if mode == "ctx":
            s_scr[...] = jnp.zeros(s_scr.shape, F32)
        else:
            s_scr[...] = s0_ref[...]

    ba = ba_ref[...]
    sig = _sigmoid(ba)
    gfull = -jnp.exp(prm_ref[0:1, :]) * _softplus(ba + prm_ref[1:2, :])
    ii = lax.broadcasted_iota(jnp.int32, (r, r), 0)
    jj = lax.broadcasted_iota(jnp.int32, (r, r), 1)
    same = (ii // CHUNK) == (jj // CHUNK)
    tri = jnp.where(jnp.logical_and(same, (jj >= ii) if rev else (jj <= ii)), 1.0, 0.0).astype(BF16)
    g_hi, g_mid, g_lo = _split3(gfull)
    gc = (jnp.dot(tri, g_hi, preferred_element_type=F32)
          + jnp.dot(tri, g_mid, preferred_element_type=F32)
          + jnp.dot(tri, g_lo, preferred_element_type=F32))
    sig_t = sig.T
    gc_t = gc.T
    for h in range(gh):
        cb = d * gh + h
        cg = 2 * gh + d * gh + h
        colb_scr[0, h] = jnp.broadcast_to(sig[:, cb:cb + 1], (r, CHUNK))
        colb_scr[1, h] = jnp.broadcast_to(gc[:, cg:cg + 1], (r, CHUNK))
        for c in range(nch):
            cols = slice(c * CHUNK, (c + 1) * CHUNK)
            rowb_scr[0, h, c] = jnp.broadcast_to(sig_t[cb:cb + 1, cols], (SUBLANES, CHUNK))
            rowb_scr[1, h, c] = jnp.broadcast_to(gc_t[cg:cg + 1, cols], (SUBLANES, CHUNK))

    ci = lax.broadcasted_iota(jnp.int32, (CHUNK, CHUNK), 0)
    cj = lax.broadcasted_iota(jnp.int32, (CHUNK, CHUNK), 1)
    low = (ci <= cj) if rev else (ci >= cj)
    strict = (ci < cj) if rev else (ci > cj)
    eye = jnp.where(ci == cj, 1.0, 0.0).astype(F32)
    last = 0 if rev else CHUNK - 1
    unroll = min(GDN_HEAD_UNROLL, gh)

    def head_group(hg, carry):
        for hh in range(unroll):
            h = hg * unroll + hh
            lanes = pl.ds(pl.multiple_of(h * hd, hd), hd)
            s = s_scr[h]
            chunks = range(nch - 1, -1, -1) if rev else range(nch)
            for c in chunks:
                rows = slice(c * CHUNK, (c + 1) * CHUNK)
                bcol = colb_scr[0, h, rows, :]
                gcol = colb_scr[1, h, rows, :]
                brow = rowb_scr[0, h, c][0:1, :]
                grow = rowb_scr[1, h, c][0:1, :]
                qc = q_ref[rows, lanes]
                kc = k_ref[rows, lanes]
                vc = v_ref[rows, lanes]
                decay = jnp.where(low, jnp.exp(jnp.where(low, gcol - grow, 0.0)), 0.0)
                gram = _mm_nt(kc, kc)
                x = -jnp.where(strict, gram * bcol * decay, 0.0)
                tmat = eye + x
                for _ in range(5):
                    x = _mm(x, x)
                    tmat = tmat + _mm(tmat, x)
                u = _mm(tmat * brow, vc)
                w = _mm(tmat * (brow * jnp.exp(grow)), kc)
                intra = _mm_nt(qc, kc) * decay
                sb = s.astype(BF16)
                vnew = u - _mm(w, sb)
                gcol1 = gcol[:, 0:1]
                glast = gcol1[last:last + 1, :]
                if mode != "ctx":
                    o = jnp.exp(gcol1) * _mm(qc, sb) + _mm(intra, vnew)
                    if mode == "fwd":
                        o_ref[rows, lanes] = o
                    else:
                        y = o + of_ref[rows, lanes]
                        y = y * lax.rsqrt(jnp.mean(y * y, axis=-1, keepdims=True) + EPS) * ng_ref[...]
                        o_ref[rows, lanes] = (y * _silu(z_ref[rows, lanes])).astype(BF16)
                s = s * jnp.exp(glast) + _mm_tn(kc, vnew * jnp.exp(glast - gcol1))
            s_scr[h] = s
        return carry

    lax.fori_loop(0, gh // unroll, head_group, 0)

    if mode == "ctx":
        @pl.when(step == nsteps - 1)
        def _():
            sfin_ref[...] = s_scr[...]


def _gdn_call(mode, rev, d, gh, qkv, ba, prm, s0=None, o_f=None, z_src=None, z_blk=None, ng=None,
              n_cols=1):
    b, r, _ = qkv.shape
    hd = LANES
    gw = gh * hd
    assert gh % min(GDN_HEAD_UNROLL, gh) == 0 and r % CHUNK == 0 and r % LANES == 0
    col = (lambda bb, w: n_cols - 1 - w) if rev else (lambda bb, w: w)
    in_specs = [pl.BlockSpec((None, r, gw), lambda bb, w: (bb, 0, 3 * col(bb, w))),
                pl.BlockSpec((None, r, gw), lambda bb, w: (bb, 0, 3 * col(bb, w) + 1)),
                pl.BlockSpec((None, r, gw), lambda bb, w: (bb, 0, 3 * col(bb, w) + 2)),
                pl.BlockSpec((None, r, LANES), lambda bb, w: (bb, 0, col(bb, w))),
                pl.BlockSpec((SUBLANES, LANES), lambda bb, w: (0, 0))]
    args = [qkv, qkv, qkv, ba, prm]
    state_spec = pl.BlockSpec((None, gh, hd, hd), lambda bb, w: (bb, 0, 0, 0))
    if mode != "ctx":
        in_specs.append(state_spec)
        args.append(s0)
    if mode == "bwd":
        zper, zoff = z_blk
        in_specs += [pl.BlockSpec((None, r, gw), lambda bb, w: (bb, 0, col(bb, w))),
                     pl.BlockSpec((None, r, gw), lambda bb, w: (bb, 0, zper * col(bb, w) + zoff)),
                     pl.BlockSpec((1, hd), lambda bb, w: (0, 0))]
        args += [o_f, z_src, ng]
    if mode == "ctx":
        out_shape = jax.ShapeDtypeStruct((b, gh, hd, hd), F32)
        out_specs = state_spec
    else:
        out_shape = jax.ShapeDtypeStruct((b, r, n_cols * gw), F32 if mode == "fwd" else BF16)
        out_specs = pl.BlockSpec((None, r, gw), lambda bb, w: (bb, 0, col(bb, w)))
    return pl.pallas_call(
        functools.partial(_gdn_kernel, mode, rev, d, gh),
        out_shape=out_shape,
        grid=(b, n_cols),
        in_specs=in_specs,
        out_specs=out_specs,
        scratch_shapes=[pltpu.VMEM((gh, hd, hd), F32),
                        pltpu.VMEM((2, gh, r // CHUNK, SUBLANES, CHUNK), F32),
                        pltpu.VMEM((2, gh, r, CHUNK), F32)],
        compiler_params=_cparams(("parallel", "arbitrary")),
        name="gdn_%s_%d" % (mode, d),
    )(*args)


def _outproj_kernel(gl_ref, yg_ref, ng_ref, w_ref, o_ref, a_scr):
    tm, lw = gl_ref.shape

    @pl.when(pl.program_id(1) == 0)
    def _():
        ng = ng_ref[...]

        def body(r, carry):
            rows = pl.ds(pl.multiple_of(r * ROW_CHUNK, ROW_CHUNK), ROW_CHUNK)
            x = gl_ref[rows, :]
            y = x * lax.rsqrt(jnp.mean(x * x, axis=-1, keepdims=True) + EPS) * ng
            a_scr[rows, 0:lw] = y.astype(BF16)
            return carry

        lax.fori_loop(0, tm // ROW_CHUNK, body, 0)
        a_scr[:, lw:] = yg_ref[...]

    o_ref[...] = jnp.dot(a_scr[...], w_ref[...], preferred_element_type=F32)


def _outproj(g_lru, y_gdn, ng, w_out):
    m, lw = g_lru.shape
    gw = y_gdn.shape[1]
    k, n = w_out.shape
    tm = _tile(m, 512, SUBLANES)
    tn = _tile(n, 1024)
    return pl.pallas_call(
        _outproj_kernel,
        out_shape=jax.ShapeDtypeStruct((m, n), F32),
        grid=(m // tm, n // tn),
        in_specs=[pl.BlockSpec((tm, lw), lambda i, j: (i, 0)),
                  pl.BlockSpec((tm, gw), lambda i, j: (i, 0)),
                  pl.BlockSpec((1, lw), lambda i, j: (0, 0)),
                  pl.BlockSpec((k, tn), lambda i, j: (0, j))],
        out_specs=pl.BlockSpec((tm, tn), lambda i, j: (i, j)),
        scratch_shapes=[pltpu.VMEM((tm, k), BF16)],
        compiler_params=_cparams(("parallel", "arbitrary")),
        name="outproj",
    )(g_lru, y_gdn, ng, w_out)


def _post_kernel(with_next, *refs):
    if with_next:
        x_ref, y_ref, gp_ref, gate_ref, gn_ref, sh_ref, sc_ref, xo_ref, ho_ref = refs
    else:
        x_ref, y_ref, gp_ref, gate_ref, xo_ref = refs
    y = y_ref[...]
    yn = y * lax.rsqrt(jnp.mean(y * y, axis=-1, keepdims=True) + EPS) * gp_ref[...]
    x1 = x_ref[...] + gate_ref[...] * yn
    xo_ref[...] = x1
    if with_next:
        ho_ref[...] = _norm_mod_rows(x1, gn_ref[...], sc_ref[...], sh_ref[...]).astype(BF16)


def _post(x, y, g_post, gate, nxt=None):
    b, t, d = x.shape
    tr = _tile(t, 256, SUBLANES)
    row = pl.BlockSpec((None, tr, d), lambda bb, i: (bb, i, 0))
    vec = pl.BlockSpec((1, d), lambda bb, i: (0, 0))
    mod = pl.BlockSpec((None, 1, d), lambda bb, i: (bb, 0, 0))
    in_specs = [row, row, vec, mod]
    args = [x, y, g_post, gate]
    out_shape = [jax.ShapeDtypeStruct((b, t, d), F32)]
    out_specs = [row]
    if nxt is not None:
        g_next, sh, sc = nxt
        in_specs += [vec, mod, mod]
        args += [g_next, sh, sc]
        out_shape.append(jax.ShapeDtypeStruct((b, t, d), BF16))
        out_specs.append(row)
    return pl.pallas_call(
        functools.partial(_post_kernel, nxt is not None),
        out_shape=tuple(out_shape),
        grid=(b, t // tr),
        in_specs=in_specs,
        out_specs=tuple(out_specs),
        compiler_params=_cparams(("parallel", "parallel")),
        name="post_next" if nxt is not None else "post",
    )(*args)


def _ffn_kernel(h_ref, wg_ref, wu_ref, wd_ref, o_ref):
    f = pl.program_id(1)
    h = h_ref[...]
    g = jnp.dot(h, wg_ref[...], preferred_element_type=F32)
    u = jnp.dot(h, wu_ref[...], preferred_element_type=F32)
    y = jnp.dot((_silu(g) * u).astype(BF16), wd_ref[...], preferred_element_type=F32)

    @pl.when(f == 0)
    def _():
        o_ref[...] = y

    @pl.when(f > 0)
    def _():
        o_ref[...] += y


def _ffn(h, wg, wu, wd):
    m, d = h.shape
    dff = wg.shape[1]
    tm = _tile(m, 512, SUBLANES)
    tf = _tile(dff, 512)
    return pl.pallas_call(
        _ffn_kernel,
        out_shape=jax.ShapeDtypeStruct((m, d), F32),
        grid=(m // tm, dff // tf),
        in_specs=[pl.BlockSpec((tm, d), lambda i, f: (i, 0)),
                  pl.BlockSpec((d, tf), lambda i, f: (0, f)),
                  pl.BlockSpec((d, tf), lambda i, f: (0, f)),
                  pl.BlockSpec((tf, d), lambda i, f: (f, 0))],
        out_specs=pl.BlockSpec((tm, d), lambda i, f: (i, 0)),
        compiler_params=_cparams(("parallel", "arbitrary")),
        name="ffn",
    )(h, wg, wu, wd)


def _pad_cols(w, mult):
    n = w.shape[-1]
    padn = (-n) % mult
    return w if padn == 0 else jnp.pad(w, ((0, 0), (0, padn)))


def kernel(x, c, ctx, c_ctx, w_ada, b_ada, g_pre_mix, g_post_mix, g_pre_ffn, g_post_ffn, w_in,
           lru_conv_w, lru_conv_b, lru_w_a, lru_b_a, lru_w_x, lru_b_x, lru_lambda, lru_norm_g,
           gdn_conv_w, gdn_a_log, gdn_dt_bias, gdn_norm_g, w_out, w_ffn_gate, w_ffn_up, w_ffn_down):
    depth = w_ada.shape[0]
    bsz, seq, d = x.shape
    tctx = ctx.shape[1]
    lw = lru_lambda.shape[-1]
    gh = gdn_a_log.shape[-1]
    hd = gdn_norm_g.shape[-1]
    gw = gh * hd
    n_main = 2 * lw + 4 * gw
    rows = seq // GRID_W
    assert hd == LANES and 4 * gh <= LANES and seq == rows * GRID_W
    assert n_main % gw == 0 and (2 * lw + 3 * gw) % gw == 0

    for l in range(depth):
        assert l == depth - 1, "context stream updates (depth > 1) are not implemented"

        crows = jnp.zeros((SUBLANES, d), F32).at[0:bsz].set(c).at[bsz].set(c_ctx)
        mod = _ada(crows, w_ada[l], b_ada[l][None, :])
        mod_lat = mod[0:bsz].reshape(bsz, N_MOD, 1, d)
        shift_m, scale_m, gate_m, shift_f, scale_f, gate_f = (mod_lat[:, i] for i in range(N_MOD))
        mod_ctx = mod[bsz].reshape(N_MOD, 1, 1, d)

        w_main = w_in[l][:, :n_main].astype(BF16)
        w_ba = _pad_cols(w_in[l][:, n_main:], LANES).astype(BF16)
        g_pre = g_pre_mix[l][None, :]
        p_lat, ba_lat = _inproj(x, g_pre, shift_m, scale_m, w_main, w_ba)
        p_ctx, ba_ctx = _inproj(ctx, g_pre, mod_ctx[0], mod_ctx[1], w_main, w_ba)

        g_lru = _lru(p_lat, p_ctx, lru_conv_w[l], lru_conv_b[l][None, :], lru_w_a[l].astype(BF16),
                     lru_b_a[l], lru_w_x[l].astype(BF16), lru_b_x[l], lru_lambda[l], lw)

        qkv_lat = _gdnpre(p_lat, 2 * lw, gdn_conv_w[l], gh, GRID_W)
        qkv_ctx = _gdnpre(p_ctx, 2 * lw, gdn_conv_w[l], gh, 1)
        prm = jnp.zeros((SUBLANES, LANES), F32)
        prm = prm.at[0, 2 * gh:4 * gh].set(gdn_a_log[l].reshape(-1))
        prm = prm.at[1, 2 * gh:4 * gh].set(gdn_dt_bias[l].reshape(-1))
        qkv_v = qkv_lat.reshape(bsz, rows, GRID_W * 3 * gw)
        ba_v = ba_lat.reshape(bsz, rows, GRID_W * LANES)
        p_v = p_lat.reshape(bsz, rows, GRID_W * n_main)
        ng = gdn_norm_g[l][None, :]
        s_f = _gdn_call("ctx", False, 0, gh, qkv_ctx, ba_ctx, prm)
        s_b = _gdn_call("ctx", True, 1, gh, qkv_ctx, ba_ctx, prm)
        o_f = _gdn_call("fwd", False, 0, gh, qkv_v, ba_v, prm, s0=s_f, n_cols=GRID_W)
        y_gdn = _gdn_call("bwd", True, 1, gh, qkv_v, ba_v, prm, s0=s_b, o_f=o_f, z_src=p_v,
                          z_blk=(n_main // gw, (2 * lw + 3 * gw) // gw), ng=ng, n_cols=GRID_W)
        y_gdn = y_gdn.reshape(bsz * seq, gw)

        mix = _outproj(g_lru.reshape(bsz * seq, lw), y_gdn, lru_norm_g[l][None, :],
                       w_out[l].astype(BF16))
        x1, h_ffn = _post(x, mix.reshape(bsz, seq, d), g_post_mix[l][None, :], gate_m,
                          nxt=(g_pre_ffn[l][None, :], shift_f, scale_f))
        dff_mult = 512
        wg = _pad_cols(w_ffn_gate[l], dff_mult).astype(BF16)
        wu = _pad_cols(w_ffn_up[l], dff_mult).astype(BF16)
        wd = jnp.pad(w_ffn_down[l], ((0, wg.shape[1] - w_ffn_down[l].shape[0]), (0, 0))).astype(BF16)
        y_ffn = _ffn(h_ffn.reshape(bsz * seq, d), wg, wu, wd)
        x = _post(x1, y_ffn.reshape(bsz, seq, d), g_post_ffn[l][None, :], gate_f)[0]
    return x
```

```python
import functools

import jax
import jax.numpy as jnp
from jax import lax
from jax.experimental import pallas as pl
from jax.experimental.pallas import tpu as pltpu

F32 = jnp.float32
BF16 = jnp.bfloat16

EPS = 1e-6
GRID_W = 64
CHUNK = 64
LRU_C = 8.0
CONV_W = 4
N_MOD = 6
LANES = 128
SUBLANES = 8
VMEM_LIMIT = 56 * 1024 * 1024


def _cparams(sem):
    return pltpu.CompilerParams(dimension_semantics=sem, vmem_limit_bytes=VMEM_LIMIT)


def _mm(a, b):
    return jnp.dot(a.astype(BF16), b.astype(BF16), preferred_element_type=F32)


def _mm_nt(a, b):
    return lax.dot_general(a.astype(BF16), b.astype(BF16), (((1,), (1,)), ((), ())),
                           preferred_element_type=F32)


def _mm_tn(a, b):
    return lax.dot_general(a.astype(BF16), b.astype(BF16), (((0,), (0,)), ((), ())),
                           preferred_element_type=F32)


def _sigmoid(x):
    return 1.0 / (1.0 + jnp.exp(-x))


def _silu(x):
    return x * _sigmoid(x)


def _softplus(x):
    return jnp.maximum(x, 0.0) + jnp.log1p(jnp.exp(-jnp.abs(x)))


def _tile(n, target, mult=LANES):
    if n <= target:
        return n
    t = (target // mult) * mult
    while t >= mult:
        if n % t == 0:
            return t
        t -= mult
    return n


def _ada_kernel(c_ref, w_ref, b_ref, o_ref):
    o_ref[...] = _mm(_silu(c_ref[...]), w_ref[...]) + b_ref[...]


def _ada(crows, w, b):
    r, d = crows.shape
    n = w.shape[1]
    tn = _tile(n, 512)
    return pl.pallas_call(
        _ada_kernel,
        out_shape=jax.ShapeDtypeStruct((r, n), F32),
        grid=(n // tn,),
        in_specs=[pl.BlockSpec((r, d), lambda j: (0, 0)),
                  pl.BlockSpec((d, tn), lambda j: (0, j)),
                  pl.BlockSpec((1, tn), lambda j: (0, j))],
        out_specs=pl.BlockSpec((r, tn), lambda j: (0, j)),
        compiler_params=_cparams(("arbitrary",)),
        name="ada",
    )(crows, w, b)


ROW_CHUNK = 64


def _norm_mod_rows(x, g, sc, sh):
    y = x * lax.rsqrt(jnp.mean(x * x, axis=-1, keepdims=True) + EPS) * g
    return y * (1.0 + sc) + sh


def _inproj_kernel(x_ref, g_ref, sh_ref, sc_ref, w_ref, wba_ref, o_ref, oba_ref, h_scr):
    tm = x_ref.shape[0]

    @pl.when(pl.program_id(2) == 0)
    def _():
        g = g_ref[...]
        sc = sc_ref[...]
        sh = sh_ref[...]

        def body(r, carry):
            rows = pl.ds(pl.multiple_of(r * ROW_CHUNK, ROW_CHUNK), ROW_CHUNK)
            h_scr[rows, :] = _norm_mod_rows(x_ref[rows, :], g, sc, sh).astype(BF16)
            return carry

        lax.fori_loop(0, tm // ROW_CHUNK, body, 0)
        oba_ref[...] = jnp.dot(h_scr[...], wba_ref[...], preferred_element_type=F32)

    o_ref[...] = jnp.dot(h_scr[...], w_ref[...], preferred_element_type=F32)


def _inproj(x, g, sh, sc, w_main, w_ba):
    b, t, d = x.shape
    n = w_main.shape[1]
    nba = w_ba.shape[1]
    tm = _tile(t, 512, SUBLANES)
    tn = _tile(n, 1024)
    per_batch = sh.shape[0] > 1
    mod_map = (lambda bb, i, j: (bb, 0, 0)) if per_batch else (lambda bb, i, j: (0, 0, 0))
    return pl.pallas_call(
        _inproj_kernel,
        out_shape=(jax.ShapeDtypeStruct((b, t, n), F32), jax.ShapeDtypeStruct((b, t, nba), F32)),
        grid=(b, t // tm, n // tn),
        in_specs=[pl.BlockSpec((None, tm, d), lambda bb, i, j: (bb, i, 0)),
                  pl.BlockSpec((1, d), lambda bb, i, j: (0, 0)),
                  pl.BlockSpec((None, 1, d), mod_map),
                  pl.BlockSpec((None, 1, d), mod_map),
                  pl.BlockSpec((d, tn), lambda bb, i, j: (0, j)),
                  pl.BlockSpec((d, nba), lambda bb, i, j: (0, 0))],
        out_specs=(pl.BlockSpec((None, tm, tn), lambda bb, i, j: (bb, i, j)),
                   pl.BlockSpec((None, tm, nba), lambda bb, i, j: (bb, i, 0))),
        scratch_shapes=[pltpu.VMEM((tm, d), BF16)],
        compiler_params=_cparams(("parallel", "parallel", "arbitrary")),
        name="inproj",
    )(x, g, sh, sc, w_main, w_ba)


def _fill_padded(xp_ref, x_ref, n, pad):
    zeros = jnp.zeros((pad, xp_ref.shape[1]), F32)
    xp_ref[0:pad, :] = zeros
    xp_ref[pad + n:pad + n + pad, :] = zeros
    step = min(n, 512)

    def body(i, carry):
        rows = pl.ds(pl.multiple_of(i * step, step), step)
        xp_ref[pl.ds(pl.multiple_of(pad + i * step, SUBLANES), step), :] = x_ref[rows, :]
        return carry

    lax.fori_loop(0, n // step, body, 0)


def _row_pitch(stride):
    return stride if stride == 1 else stride + SUBLANES


def _conv_rows(xp_ref, cw_ref, t0, rows, pad, stride):
    acc = None
    for j in range(CONV_W):
        off = pad + (j - 2) * stride
        term = cw_ref[j:j + 1, :] * xp_ref[pl.ds(t0 + off, rows), :]
        acc = term if acc is None else acc + term
    return acc


LRU_ROWS = 256
LRU_PAD = 8


def _lru_coeffs(xc, d, wa_ref, ba_ref, wx_ref, bx_ref, lam_ref):
    xb = xc.astype(BF16)
    r = _sigmoid(jnp.dot(xb, wa_ref[d], preferred_element_type=F32) + ba_ref[d:d + 1, :])
    i = _sigmoid(jnp.dot(xb, wx_ref[d], preferred_element_type=F32) + bx_ref[d:d + 1, :])
    log_a = (-LRU_C) * r * _softplus(-lam_ref[d:d + 1, :])
    a = jnp.exp(log_a)
    th = jnp.tanh(log_a)
    return a, jnp.sqrt(-2.0 * th / (1.0 - th)) * (i * xc)


def _lru_scan_chunk(a, b, h, rev):
    rows, c = a.shape
    nt = rows // SUBLANES
    a3 = a.reshape(nt, SUBLANES, c)
    b3 = b.reshape(nt, SUBLANES, c)
    ridx = lax.broadcasted_iota(jnp.int32, (nt, SUBLANES, c), 1)
    for s in (1, 2, 4):
        shift = (SUBLANES - s) if rev else s
        a_s = pltpu.roll(a3, shift, 1)
        b_s = pltpu.roll(b3, shift, 1)
        m = (ridx < SUBLANES - s) if rev else (ridx >= s)
        b3 = jnp.where(m, a3 * b_s + b3, b3)
        a3 = jnp.where(m, a3 * a_s, a3)
    out = [None] * nt
    order = range(nt - 1, -1, -1) if rev else range(nt)
    for t in order:
        ht = b3[t] + a3[t] * h
        out[t] = ht
        h = ht[0:1, :] if rev else ht[SUBLANES - 1:SUBLANES, :]
    return jnp.concatenate(out, axis=0), h


def _lru_kernel(ul_ref, gl_ref, uc_ref, cw_ref, cb_ref, wa_ref, ba_ref, wx_ref, bx_ref, lam_ref,
                o_ref, xpl_ref, xpc_ref):
    t_lat = ul_ref.shape[0]
    t_ctx = uc_ref.shape[0]
    c = ul_ref.shape[1]
    rl = min(LRU_ROWS, t_lat)
    rc = min(LRU_ROWS, t_ctx)
    _fill_padded(xpl_ref, ul_ref, t_lat, LRU_PAD)
    _fill_padded(xpc_ref, uc_ref, t_ctx, LRU_PAD)
    cb = cb_ref[...]

    for d in range(2):
        rev = d == 1
        prm = (wa_ref, ba_ref, wx_ref, bx_ref, lam_ref)

        def ctx_body(i, h, rev=rev, d=d, prm=prm):
            blk = (t_ctx // rc - 1 - i) if rev else i
            t0 = pl.multiple_of(blk * rc, rc)
            xc = _conv_rows(xpc_ref, cw_ref, t0, rc, LRU_PAD, 1) + cb
            a, b = _lru_coeffs(xc, d, *prm)
            _, h = _lru_scan_chunk(a, b, h, rev)
            return h

        h = lax.fori_loop(0, t_ctx // rc, ctx_body, jnp.zeros((1, c), F32))

        def lat_body(i, h, rev=rev, d=d, prm=prm):
            blk = (t_lat // rl - 1 - i) if rev else i
            t0 = pl.multiple_of(blk * rl, rl)
            rows = pl.ds(t0, rl)
            xc = _conv_rows(xpl_ref, cw_ref, t0, rl, LRU_PAD, 1) + cb
            a, b = _lru_coeffs(xc, d, *prm)
            hs, h = _lru_scan_chunk(a, b, h, rev)
            if rev:
                o_ref[rows, :] = (o_ref[rows, :] + hs) * jax.nn.gelu(gl_ref[rows, :], approximate=True)
            else:
                o_ref[rows, :] = hs
            return h

        lax.fori_loop(0, t_lat // rl, lat_body, h)


def _lru(p_lat, p_ctx, conv_w, conv_b, w_a, b_a, w_x, b_x, lam, lw):
    b, t, _ = p_lat.shape
    tc = p_ctx.shape[1]
    nb, bd = w_a.shape[1], w_a.shape[2]
    assert bd == LANES and nb * bd == lw
    vec = lambda bb, n: (0, n)
    return pl.pallas_call(
        _lru_kernel,
        out_shape=jax.ShapeDtypeStruct((b, t, lw), F32),
        grid=(b, nb),
        in_specs=[pl.BlockSpec((None, t, bd), lambda bb, n: (bb, 0, n)),
                  pl.BlockSpec((None, t, bd), lambda bb, n: (bb, 0, nb + n)),
                  pl.BlockSpec((None, tc, bd), lambda bb, n: (bb, 0, n)),
                  pl.BlockSpec((CONV_W, bd), vec),
                  pl.BlockSpec((1, bd), vec),
                  pl.BlockSpec((2, None, bd, bd), lambda bb, n: (0, n, 0, 0)),
                  pl.BlockSpec((2, bd), vec),
                  pl.BlockSpec((2, None, bd, bd), lambda bb, n: (0, n, 0, 0)),
                  pl.BlockSpec((2, bd), vec),
                  pl.BlockSpec((2, bd), vec)],
        out_specs=pl.BlockSpec((None, t, bd), lambda bb, n: (bb, 0, n)),
        scratch_shapes=[pltpu.VMEM((t + 2 * LRU_PAD, bd), F32),
                        pltpu.VMEM((tc + 2 * LRU_PAD, bd), F32)],
        compiler_params=_cparams(("parallel", "parallel")),
        name="lru",
    )(p_lat, p_lat, p_ctx, conv_w, conv_b, w_a, b_a, w_x, b_x, lam)


def _gdnpre_kernel(stride, n_heads, x_ref, cw_ref, o_ref, xp_ref, y_ref):
    t = x_ref.shape[0]
    pad = max(2 * stride, SUBLANES)
    _fill_padded(xp_ref, x_ref, t, pad)
    step = min(t, 512)
    pitch = _row_pitch(stride)
    per = step // stride if stride > 1 else 1

    def conv_body(i, carry):
        t0 = pl.multiple_of(i * step, step)
        y = _conv_rows(xp_ref, cw_ref, t0, step, pad, stride)
        if stride == 1:
            y_ref[pl.ds(t0, step), :] = y
        else:
            for j in range(per):
                dst = pl.multiple_of((i * per + j) * pitch, SUBLANES)
                y_ref[pl.ds(dst, stride), :] = y[j * stride:(j + 1) * stride, :]
        return carry

    lax.fori_loop(0, t // step, conv_body, 0)

    if stride > 1:
        w = stride
        n_rows = t // w
        last = t - w
        ylast = (n_rows - 1) * pitch
        col = lax.broadcasted_iota(jnp.int32, (w, 1), 0)
        not_first = col >= 1
        not_last = col <= w - 2
        prev_bot = xp_ref[pl.ds(pad + last - 1, w), :]
        prev_bot2 = xp_ref[pl.ds(pad + last - w - 1, w), :]
        next_top = xp_ref[pl.ds(pad + 1, w), :]
        y_ref[0:w, :] = y_ref[0:w, :] + jnp.where(
            not_first, cw_ref[1:2, :] * prev_bot + cw_ref[0:1, :] * prev_bot2, 0.0)
        y_ref[pitch:pitch + w, :] = y_ref[pitch:pitch + w, :] + jnp.where(
            not_first, cw_ref[0:1, :] * prev_bot, 0.0)
        y_ref[ylast:ylast + w, :] = y_ref[ylast:ylast + w, :] + jnp.where(
            not_last, cw_ref[3:4, :] * next_top, 0.0)

    blk = pl.program_id(1)
    hd = x_ref.shape[1]

    def finish(normed, scale):
        def act(y):
            y = _silu(y)
            if normed:
                y = y * (lax.rsqrt(jnp.sum(y * y, axis=-1, keepdims=True) + EPS) * scale)
            return y.astype(BF16)

        if stride == 1:
            def body(i, carry):
                rows = pl.ds(pl.multiple_of(i * step, step), step)
                o_ref[rows, :] = act(y_ref[rows, :])
                return carry

            lax.fori_loop(0, t // step, body, 0)
        else:
            n_rows = t // stride

            def body(w, carry):
                dst = pl.ds(pl.multiple_of(w * n_rows, n_rows), n_rows)
                o_ref[dst, :] = act(y_ref[pl.ds(w, n_rows, stride=pitch), :])
                return carry

            lax.fori_loop(0, stride, body, 0)

    @pl.when(blk < n_heads)
    def _():
        finish(True, hd ** -0.5)

    @pl.when(jnp.logical_and(blk >= n_heads, blk < 2 * n_heads))
    def _():
        finish(True, 1.0)

    @pl.when(blk >= 2 * n_heads)
    def _():
        finish(False, 1.0)


def _gdnpre(p, col0, conv_w, n_heads, stride):
    b, t, _ = p.shape
    hd = LANES
    nblk = 3 * n_heads
    assert col0 % hd == 0 and t % stride == 0
    pad = max(2 * stride, SUBLANES)
    y_rows = (t // stride) * _row_pitch(stride)
    return pl.pallas_call(
        functools.partial(_gdnpre_kernel, stride, n_heads),
        out_shape=jax.ShapeDtypeStruct((b, t, nblk * hd), BF16),
        grid=(b, nblk),
        in_specs=[pl.BlockSpec((None, t, hd), lambda bb, n: (bb, 0, col0 // hd + n)),
                  pl.BlockSpec((CONV_W, hd), lambda bb, n: (0, n))],
        out_specs=pl.BlockSpec((None, t, hd), lambda bb, n: (bb, 0, n)),
        scratch_shapes=[pltpu.VMEM((t + 2 * pad, hd), F32), pltpu.VMEM((y_rows, hd), F32)],
        compiler_params=_cparams(("parallel", "parallel")),
        name="gdnpre",
    )(p, conv_w)


GDN_GROUP_ITEMS = 16


def _split3(x):
    hi = x.astype(BF16)
    r1 = x - hi.astype(F32)
    mid = r1.astype(BF16)
    lo = (r1 - mid.astype(F32)).astype(BF16)
    return hi, mid, lo


def _gdn_kernel(mode, rev, d, gh, *refs):
    if mode == "ctx":
        q_ref, k_ref, v_ref, ba_ref, prm_ref, sfin_ref, s_scr = refs
    elif mode == "fwd":
        q_ref, k_ref, v_ref, ba_ref, prm_ref, s0_ref, o_ref, s_scr = refs
    else:
        q_ref, k_ref, v_ref, ba_ref, prm_ref, s0_ref, of_ref, o_ref, s_scr = refs
    r = q_ref.shape[0]
    hd = LANES
    nch = r // CHUNK
    step = pl.program_id(1)
    nsteps = pl.num_programs(1)

    @pl.when(step == 0)
    def _():
        if mode == "ctx":
            s_scr[...] = jnp.zeros(s_scr.shape, F32)
        else:
            s_scr[...] = s0_ref[...]

    ba = ba_ref[...]
    sig = _sigmoid(ba)
    gfull = -jnp.exp(prm_ref[0:1, :]) * _softplus(ba + prm_ref[1:2, :])
    ii = lax.broadcasted_iota(jnp.int32, (r, r), 0)
    jj = lax.broadcasted_iota(jnp.int32, (r, r), 1)
    same = (ii // CHUNK) == (jj // CHUNK)
    tri = jnp.where(jnp.logical_and(same, (jj >= ii) if rev else (jj <= ii)), 1.0, 0.0).astype(BF16)
    g_hi, g_mid, g_lo = _split3(gfull)
    gc = (jnp.dot(tri, g_hi, preferred_element_type=F32)
          + jnp.dot(tri, g_mid, preferred_element_type=F32)
          + jnp.dot(tri, g_lo, preferred_element_type=F32))
    sig_t = sig.T
    gc_t = gc.T
    egc = jnp.exp(gc)

    ci = lax.broadcasted_iota(jnp.int32, (CHUNK, CHUNK), 0)
    cj = lax.broadcasted_iota(jnp.int32, (CHUNK, CHUNK), 1)
    low = (ci <= cj) if rev else (ci >= cj)
    strict = (ci < cj) if rev else (ci > cj)
    eye = jnp.where(ci == cj, 1.0, 0.0).astype(F32)
    last = 0 if rev else CHUNK - 1
    chunk_order = list(range(nch - 1, -1, -1) if rev else range(nch))
    heads_per_group = max(1, min(gh, GDN_GROUP_ITEMS // nch))

    for g0 in range(0, gh, heads_per_group):
        heads = list(range(g0, min(gh, g0 + heads_per_group)))
        items = [(h, c) for h in heads for c in range(nch)]
        st = {}
        for (h, c) in items:
            rows = slice(c * CHUNK, (c + 1) * CHUNK)
            lanes = slice(h * hd, (h + 1) * hd)
            cb = d * gh + h
            cg = 2 * gh + d * gh + h
            bcol = sig[rows, cb:cb + 1]
            gcol = gc[rows, cg:cg + 1]
            brow = sig_t[cb:cb + 1, rows]
            grow = gc_t[cg:cg + 1, rows]
            kc = k_ref[rows, lanes]
            decay = jnp.where(low, jnp.exp(jnp.where(low, gcol - grow, 0.0)), 0.0)
            x = -jnp.where(strict, _mm_nt(kc, kc) * bcol * decay, 0.0)
            it = dict(rows=rows, lanes=lanes, kc=kc, gcol=gcol, brow=brow, grow=grow,
                      egcol=egc[rows, cg:cg + 1], xb=x.astype(BF16), t=eye + x)
            if mode != "ctx":
                it["intra"] = (_mm_nt(q_ref[rows, lanes], kc) * decay).astype(BF16)
            st[(h, c)] = it
        for _ in range(5):
            for it in st.values():
                it["xb"] = jnp.dot(it["xb"], it["xb"], preferred_element_type=F32).astype(BF16)
            for it in st.values():
                it["t"] = it["t"] + jnp.dot(it["t"].astype(BF16), it["xb"],
                                            preferred_element_type=F32)
        for it in st.values():
            tb = it["t"] * it["brow"]
            it["u"] = _mm(tb, v_ref[it["rows"], it["lanes"]])
            it["w"] = _mm(tb * jnp.exp(it["grow"]), it["kc"]).astype(BF16)
            del it["t"], it["xb"]
        state = {h: s_scr[h] for h in heads}
        for c in chunk_order:
            for h in heads:
                it = st[(h, c)]
                s = state[h]
                sb = s.astype(BF16)
                vnew = it["u"] - jnp.dot(it["w"], sb, preferred_element_type=F32)
                glast = it["gcol"][last:last + 1, :]
                if mode != "ctx":
                    o = (it["egcol"] * jnp.dot(q_ref[it["rows"], it["lanes"]], sb,
                                               preferred_element_type=F32)
                         + _mm(it["intra"], vnew))
                    if mode == "bwd":
                        o = o + of_ref[it["rows"], it["lanes"]]
                    o_ref[it["rows"], it["lanes"]] = o
                state[h] = s * jnp.exp(glast) + _mm_tn(it["kc"], vnew * jnp.exp(glast - it["gcol"]))
        for h in heads:
            s_scr[h] = state[h]

    if mode == "ctx":
        @pl.when(step == nsteps - 1)
        def _():
            sfin_ref[...] = s_scr[...]


def _gdn_call(mode, rev, d, gh, qkv, ba, prm, r, s0=None, o_f=None):
    b, t, _ = qkv.shape
    hd = LANES
    gw = gh * hd
    n_blk = t // r
    assert t % r == 0 and r % CHUNK == 0 and r % LANES == 0
    blk = (lambda w: n_blk - 1 - w) if rev else (lambda w: w)
    in_specs = [pl.BlockSpec((None, r, gw), lambda bb, w: (bb, blk(w), 0)),
                pl.BlockSpec((None, r, gw), lambda bb, w: (bb, blk(w), 1)),
                pl.BlockSpec((None, r, gw), lambda bb, w: (bb, blk(w), 2)),
                pl.BlockSpec((None, r, LANES), lambda bb, w: (bb, blk(w), 0)),
                pl.BlockSpec((SUBLANES, LANES), lambda bb, w: (0, 0))]
    args = [qkv, qkv, qkv, ba, prm]
    state_spec = pl.BlockSpec((None, gh, hd, hd), lambda bb, w: (bb, 0, 0, 0))
    o_spec = pl.BlockSpec((None, r, gw), lambda bb, w: (bb, blk(w), 0))
    if mode != "ctx":
        in_specs.append(state_spec)
        args.append(s0)
    if mode == "bwd":
        in_specs.append(o_spec)
        args.append(o_f)
    if mode == "ctx":
        out_shape = jax.ShapeDtypeStruct((b, gh, hd, hd), F32)
        out_specs = state_spec
    else:
        out_shape = jax.ShapeDtypeStruct((b, t, gw), F32)
        out_specs = o_spec
    return pl.pallas_call(
        functools.partial(_gdn_kernel, mode, rev, d, gh),
        out_shape=out_shape,
        grid=(b, n_blk),
        in_specs=in_specs,
        out_specs=out_specs,
        scratch_shapes=[pltpu.VMEM((gh, hd, hd), F32)],
        compiler_params=_cparams(("parallel", "arbitrary")),
        name="gdn_%s_%d" % (mode, d),
    )(*args)


def _to_visit_kernel(width, x_ref, o_ref):
    n_rows = x_ref.shape[0] // width

    def body(w, carry):
        dst = pl.ds(pl.multiple_of(w * n_rows, n_rows), n_rows)
        o_ref[dst, :] = x_ref[pl.ds(w, n_rows, stride=width), :]
        return carry

    lax.fori_loop(0, width, body, 0)


def _to_visit(x, width):
    b, t, c = x.shape
    assert c == LANES
    spec = pl.BlockSpec((None, t, c), lambda bb: (bb, 0, 0))
    return pl.pallas_call(
        functools.partial(_to_visit_kernel, width),
        out_shape=jax.ShapeDtypeStruct((b, t, c), F32),
        grid=(b,),
        in_specs=[spec],
        out_specs=spec,
        compiler_params=_cparams(("parallel",)),
        name="to_visit",
    )(x)


def _gdnpost_kernel(width, o_ref, z_ref, ng_ref, y_ref, scr_ref):
    t = o_ref.shape[0]
    n_rows = t // width
    pitch = _row_pitch(width)

    def scatter(w, carry):
        src = pl.ds(pl.multiple_of(w * n_rows, n_rows), n_rows)
        scr_ref[pl.ds(w, n_rows, stride=pitch), :] = o_ref[src, :]
        return carry

    lax.fori_loop(0, width, scatter, 0)
    per = SUBLANES
    ng = ng_ref[...]

    def body(i, carry):
        parts = [scr_ref[pl.ds(pl.multiple_of((i * per + j) * pitch, SUBLANES), width), :]
                 for j in range(per)]
        o = jnp.concatenate(parts, axis=0)
        rows = pl.ds(pl.multiple_of(i * per * width, per * width), per * width)
        y = o * lax.rsqrt(jnp.mean(o * o, axis=-1, keepdims=True) + EPS) * ng
        y_ref[rows, :] = (y * _silu(z_ref[rows, :])).astype(BF16)
        return carry

    lax.fori_loop(0, n_rows // per, body, 0)


def _gdnpost(o, p, zcol0, ng, gh, width):
    b, t, gw = o.shape
    hd = LANES
    n_rows = t // width
    assert zcol0 % hd == 0 and n_rows % SUBLANES == 0
    return pl.pallas_call(
        functools.partial(_gdnpost_kernel, width),
        out_shape=jax.ShapeDtypeStruct((b, t, gw), BF16),
        grid=(b, gh),
        in_specs=[pl.BlockSpec((None, t, hd), lambda bb, h: (bb, 0, h)),
                  pl.BlockSpec((None, t, hd), lambda bb, h: (bb, 0, zcol0 // hd + h)),
                  pl.BlockSpec((1, hd), lambda bb, h: (0, 0))],
        out_specs=pl.BlockSpec((None, t, hd), lambda bb, h: (bb, 0, h)),
        scratch_shapes=[pltpu.VMEM((n_rows * _row_pitch(width), hd), F32)],
        compiler_params=_cparams(("parallel", "parallel")),
        name="gdnpost",
    )(o, p, ng)


def _outproj_kernel(gl_ref, yg_ref, ng_ref, w_ref, o_ref, a_scr):
    tm, lw = gl_ref.shape

    @pl.when(pl.program_id(1) == 0)
    def _():
        ng = ng_ref[...]

        def body(r, carry):
            rows = pl.ds(pl.multiple_of(r * ROW_CHUNK, ROW_CHUNK), ROW_CHUNK)
            x = gl_ref[rows, :]
            y = x * lax.rsqrt(jnp.mean(x * x, axis=-1, keepdims=True) + EPS) * ng
            a_scr[rows, 0:lw] = y.astype(BF16)
            return carry

        lax.fori_loop(0, tm // ROW_CHUNK, body, 0)
        a_scr[:, lw:] = yg_ref[...]

    o_ref[...] = jnp.dot(a_scr[...], w_ref[...], preferred_element_type=F32)


def _outproj(g_lru, y_gdn, ng, w_out):
    m, lw = g_lru.shape
    gw = y_gdn.shape[1]
    k, n = w_out.shape
    tm = _tile(m, 512, SUBLANES)
    tn = _tile(n, 1024)
    return pl.pallas_call(
        _outproj_kernel,
        out_shape=jax.ShapeDtypeStruct((m, n), F32),
        grid=(m // tm, n // tn),
        in_specs=[pl.BlockSpec((tm, lw), lambda i, j: (i, 0)),
                  pl.BlockSpec((tm, gw), lambda i, j: (i, 0)),
                  pl.BlockSpec((1, lw), lambda i, j: (0, 0)),
                  pl.BlockSpec((k, tn), lambda i, j: (0, j))],
        out_specs=pl.BlockSpec((tm, tn), lambda i, j: (i, j)),
        scratch_shapes=[pltpu.VMEM((tm, k), BF16)],
        compiler_params=_cparams(("parallel", "arbitrary")),
        name="outproj",
    )(g_lru, y_gdn, ng, w_out)


def _post_kernel(with_next, *refs):
    if with_next:
        x_ref, y_ref, gp_ref, gate_ref, gn_ref, sh_ref, sc_ref, xo_ref, ho_ref = refs
    else:
        x_ref, y_ref, gp_ref, gate_ref, xo_ref = refs
    y = y_ref[...]
    yn = y * lax.rsqrt(jnp.mean(y * y, axis=-1, keepdims=True) + EPS) * gp_ref[...]
    x1 = x_ref[...] + gate_ref[...] * yn
    xo_ref[...] = x1
    if with_next:
        ho_ref[...] = _norm_mod_rows(x1, gn_ref[...], sc_ref[...], sh_ref[...]).astype(BF16)


def _post(x, y, g_post, gate, nxt=None):
    b, t, d = x.shape
    tr = _tile(t, 256, SUBLANES)
    row = pl.BlockSpec((None, tr, d), lambda bb, i: (bb, i, 0))
    vec = pl.BlockSpec((1, d), lambda bb, i: (0, 0))
    mod = pl.BlockSpec((None, 1, d), lambda bb, i: (bb, 0, 0))
    in_specs = [row, row, vec, mod]
    args = [x, y, g_post, gate]
    out_shape = [jax.ShapeDtypeStruct((b, t, d), F32)]
    out_specs = [row]
    if nxt is not None:
        g_next, sh, sc = nxt
        in_specs += [vec, mod, mod]
        args += [g_next, sh, sc]
        out_shape.append(jax.ShapeDtypeStruct((b, t, d), BF16))
        out_specs.append(row)
    return pl.pallas_call(
        functools.partial(_post_kernel, nxt is not None),
        out_shape=tuple(out_shape),
        grid=(b, t // tr),
        in_specs=in_specs,
        out_specs=tuple(out_specs),
        compiler_params=_cparams(("parallel", "parallel")),
        name="post_next" if nxt is not None else "post",
    )(*args)


def _ffn_kernel(h_ref, wg_ref, wu_ref, wd_ref, o_ref):
    f = pl.program_id(1)
    h = h_ref[...]
    g = jnp.dot(h, wg_ref[...], preferred_element_type=F32)
    u = jnp.dot(h, wu_ref[...], preferred_element_type=F32)
    y = jnp.dot((_silu(g) * u).astype(BF16), wd_ref[...], preferred_element_type=F32)

    @pl.when(f == 0)
    def _():
        o_ref[...] = y

    @pl.when(f > 0)
    def _():
        o_ref[...] += y


def _ffn(h, wg, wu, wd):
    m, d = h.shape
    dff = wg.shape[1]
    tm = _tile(m, 512, SUBLANES)
    tf = _tile(dff, 512)
    return pl.pallas_call(
        _ffn_kernel,
        out_shape=jax.ShapeDtypeStruct((m, d), F32),
        grid=(m // tm, dff // tf),
        in_specs=[pl.BlockSpec((tm, d), lambda i, f: (i, 0)),
                  pl.BlockSpec((d, tf), lambda i, f: (0, f)),
                  pl.BlockSpec((d, tf), lambda i, f: (0, f)),
                  pl.BlockSpec((tf, d), lambda i, f: (f, 0))],
        out_specs=pl.BlockSpec((tm, d), lambda i, f: (i, 0)),
        compiler_params=_cparams(("parallel", "arbitrary")),
        name="ffn",
    )(h, wg, wu, wd)


def _pad_cols(w, mult):
    n = w.shape[-1]
    padn = (-n) % mult
    return w if padn == 0 else jnp.pad(w, ((0, 0), (0, padn)))


def kernel(x, c, ctx, c_ctx, w_ada, b_ada, g_pre_mix, g_post_mix, g_pre_ffn, g_post_ffn, w_in,
           lru_conv_w, lru_conv_b, lru_w_a, lru_b_a, lru_w_x, lru_b_x, lru_lambda, lru_norm_g,
           gdn_conv_w, gdn_a_log, gdn_dt_bias, gdn_norm_g, w_out, w_ffn_gate, w_ffn_up, w_ffn_down):
    depth = w_ada.shape[0]
    bsz, seq, d = x.shape
    tctx = ctx.shape[1]
    lw = lru_lambda.shape[-1]
    gh = gdn_a_log.shape[-1]
    hd = gdn_norm_g.shape[-1]
    gw = gh * hd
    n_main = 2 * lw + 4 * gw
    rows = seq // GRID_W
    assert hd == LANES and 4 * gh <= LANES and seq == rows * GRID_W
    assert n_main % gw == 0 and (2 * lw + 3 * gw) % gw == 0

    for l in range(depth):
        assert l == depth - 1, "context stream updates (depth > 1) are not implemented"

        crows = jnp.zeros((SUBLANES, d), F32).at[0:bsz].set(c).at[bsz].set(c_ctx)
        mod = _ada(crows, w_ada[l], b_ada[l][None, :])
        mod_lat = mod[0:bsz].reshape(bsz, N_MOD, 1, d)
        shift_m, scale_m, gate_m, shift_f, scale_f, gate_f = (mod_lat[:, i] for i in range(N_MOD))
        mod_ctx = mod[bsz].reshape(N_MOD, 1, 1, d)

        w_main = w_in[l][:, :n_main].astype(BF16)
        w_ba = _pad_cols(w_in[l][:, n_main:], LANES).astype(BF16)
        g_pre = g_pre_mix[l][None, :]
        p_lat, ba_lat = _inproj(x, g_pre, shift_m, scale_m, w_main, w_ba)
        p_ctx, ba_ctx = _inproj(ctx, g_pre, mod_ctx[0], mod_ctx[1], w_main, w_ba)

        g_lru = _lru(p_lat, p_ctx, lru_conv_w[l], lru_conv_b[l][None, :], lru_w_a[l].astype(BF16),
                     lru_b_a[l], lru_w_x[l].astype(BF16), lru_b_x[l], lru_lambda[l], lw)

        qkv_lat = _gdnpre(p_lat, 2 * lw, gdn_conv_w[l], gh, GRID_W)
        qkv_ctx = _gdnpre(p_ctx, 2 * lw, gdn_conv_w[l], gh, 1)
        prm = jnp.zeros((SUBLANES, LANES), F32)
        prm = prm.at[0, 2 * gh:4 * gh].set(gdn_a_log[l].reshape(-1))
        prm = prm.at[1, 2 * gh:4 * gh].set(gdn_dt_bias[l].reshape(-1))
        ba_vis = _to_visit(ba_lat, GRID_W)
        s_f = _gdn_call("ctx", False, 0, gh, qkv_ctx, ba_ctx, prm, tctx)
        s_b = _gdn_call("ctx", True, 1, gh, qkv_ctx, ba_ctx, prm, tctx)
        o_f = _gdn_call("fwd", False, 0, gh, qkv_lat, ba_vis, prm, rows, s0=s_f)
        o_fb = _gdn_call("bwd", True, 1, gh, qkv_lat, ba_vis, prm, rows, s0=s_b, o_f=o_f)
        y_gdn = _gdnpost(o_fb, p_lat, 2 * lw + 3 * gw, gdn_norm_g[l][None, :], gh, GRID_W)
        y_gdn = y_gdn.reshape(bsz * seq, gw)

        mix = _outproj(g_lru.reshape(bsz * seq, lw), y_gdn, lru_norm_g[l][None, :],
                       w_out[l].astype(BF16))
        x1, h_ffn = _post(x, mix.reshape(bsz, seq, d), g_post_mix[l][None, :], gate_m,
                          nxt=(g_pre_ffn[l][None, :], shift_f, scale_f))
        dff_mult = 512
        wg = _pad_cols(w_ffn_gate[l], dff_mult).astype(BF16)
        wu = _pad_cols(w_ffn_up[l], dff_mult).astype(BF16)
        wd = jnp.pad(w_ffn_down[l], ((0, wg.shape[1] - w_ffn_down[l].shape[0]), (0, 0))).astype(BF16)
        y_ffn = _ffn(h_ffn.reshape(bsz * seq, d), wg, wu, wd)
        x = _post(x1, y_ffn.reshape(bsz, seq, d), g_post_ffn[l][None, :], gate_f)[0]
    return x
```

```python
import functools

import jax
import jax.numpy as jnp
from jax import lax
from jax.experimental import pallas as pl
from jax.experimental.pallas import tpu as pltpu

F32 = jnp.float32
BF16 = jnp.bfloat16

EPS = 1e-6
GRID_W = 64
CHUNK = 64
LRU_C = 8.0
CONV_W = 4
N_MOD = 6
LANES = 128
SUBLANES = 8
VMEM_LIMIT = 56 * 1024 * 1024


def _cparams(sem):
    return pltpu.CompilerParams(dimension_semantics=sem, vmem_limit_bytes=VMEM_LIMIT)


def _mm(a, b):
    return jnp.dot(a.astype(BF16), b.astype(BF16), preferred_element_type=F32)


def _mm_nt(a, b):
    return lax.dot_general(a.astype(BF16), b.astype(BF16), (((1,), (1,)), ((), ())),
                           preferred_element_type=F32)


def _mm_tn(a, b):
    return lax.dot_general(a.astype(BF16), b.astype(BF16), (((0,), (0,)), ((), ())),
                           preferred_element_type=F32)


def _sigmoid(x):
    return 1.0 / (1.0 + jnp.exp(-x))


def _silu(x):
    return x * _sigmoid(x)


def _softplus(x):
    return jnp.maximum(x, 0.0) + jnp.log1p(jnp.exp(-jnp.abs(x)))


def _tile(n, target, mult=LANES):
    if n <= target:
        return n
    t = (target // mult) * mult
    while t >= mult:
        if n % t == 0:
            return t
        t -= mult
    return n


def _ada_kernel(c_ref, w_ref, b_ref, o_ref):
    o_ref[...] = _mm(_silu(c_ref[...]), w_ref[...]) + b_ref[...]


def _ada(crows, w, b):
    r, d = crows.shape
    n = w.shape[1]
    tn = _tile(n, 512)
    return pl.pallas_call(
        _ada_kernel,
        out_shape=jax.ShapeDtypeStruct((r, n), F32),
        grid=(n // tn,),
        in_specs=[pl.BlockSpec((r, d), lambda j: (0, 0)),
                  pl.BlockSpec((d, tn), lambda j: (0, j)),
                  pl.BlockSpec((1, tn), lambda j: (0, j))],
        out_specs=pl.BlockSpec((r, tn), lambda j: (0, j)),
        compiler_params=_cparams(("arbitrary",)),
        name="ada",
    )(crows, w, b)


ROW_CHUNK = 64


def _norm_mod_rows(x, g, sc, sh):
    y = x * lax.rsqrt(jnp.mean(x * x, axis=-1, keepdims=True) + EPS) * g
    return y * (1.0 + sc) + sh


def _prenorm_kernel(x_ref, g_ref, sh_ref, sc_ref, o_ref):
    o_ref[...] = _norm_mod_rows(x_ref[...], g_ref[...], sc_ref[...], sh_ref[...]).astype(BF16)


def _prenorm(x, g, sh, sc):
    b, t, d = x.shape
    tr = _tile(t, 256, SUBLANES)
    per_batch = sh.shape[0] > 1
    mod_map = (lambda bb, i: (bb, 0, 0)) if per_batch else (lambda bb, i: (0, 0, 0))
    row = pl.BlockSpec((None, tr, d), lambda bb, i: (bb, i, 0))
    return pl.pallas_call(
        _prenorm_kernel,
        out_shape=jax.ShapeDtypeStruct((b, t, d), BF16),
        grid=(b, t // tr),
        in_specs=[row, pl.BlockSpec((1, d), lambda bb, i: (0, 0)),
                  pl.BlockSpec((None, 1, d), mod_map), pl.BlockSpec((None, 1, d), mod_map)],
        out_specs=row,
        compiler_params=_cparams(("parallel", "parallel")),
        name="prenorm",
    )(x, g, sh, sc)


def _inproj_kernel(h_ref, w_ref, wba_ref, o_ref, oba_ref):
    @pl.when(pl.program_id(2) == 0)
    def _():
        oba_ref[...] = jnp.dot(h_ref[...], wba_ref[...], preferred_element_type=F32)

    o_ref[...] = jnp.dot(h_ref[...], w_ref[...], preferred_element_type=F32)


def _inproj(h, w, n, w_ba):
    b, t, d = h.shape
    nba = w_ba.shape[1]
    tm = _tile(t, 1024, SUBLANES)
    tn = _tile(n, 768)
    return pl.pallas_call(
        _inproj_kernel,
        out_shape=(jax.ShapeDtypeStruct((b, t, n), F32), jax.ShapeDtypeStruct((b, t, nba), F32)),
        grid=(b, t // tm, n // tn),
        in_specs=[pl.BlockSpec((None, tm, d), lambda bb, i, j: (bb, i, 0)),
                  pl.BlockSpec((d, tn), lambda bb, i, j: (0, j)),
                  pl.BlockSpec((d, nba), lambda bb, i, j: (0, 0))],
        out_specs=(pl.BlockSpec((None, tm, tn), lambda bb, i, j: (bb, i, j)),
                   pl.BlockSpec((None, tm, nba), lambda bb, i, j: (bb, i, 0))),
        compiler_params=_cparams(("parallel", "parallel", "arbitrary")),
        name="inproj",
    )(h, w, w_ba)


def _fill_padded(xp_ref, x_ref, n, pad):
    zeros = jnp.zeros((pad, xp_ref.shape[1]), F32)
    xp_ref[0:pad, :] = zeros
    xp_ref[pad + n:pad + n + pad, :] = zeros
    step = min(n, 512)

    def body(i, carry):
        rows = pl.ds(pl.multiple_of(i * step, step), step)
        xp_ref[pl.ds(pl.multiple_of(pad + i * step, SUBLANES), step), :] = x_ref[rows, :]
        return carry

    lax.fori_loop(0, n // step, body, 0)


def _row_pitch(stride):
    return stride if stride == 1 else stride + SUBLANES


def _conv_rows(xp_ref, cw_ref, t0, rows, pad, stride):
    acc = None
    for j in range(CONV_W):
        off = pad + (j - 2) * stride
        term = cw_ref[j:j + 1, :] * xp_ref[pl.ds(t0 + off, rows), :]
        acc = term if acc is None else acc + term
    return acc


LRU_ROWS = 512
LRU_PAD = 8


def _lru_coeffs(xc, d, wa_ref, ba_ref, wx_ref, bx_ref, lam_ref):
    xb = xc.astype(BF16)
    r = _sigmoid(jnp.dot(xb, wa_ref[d], preferred_element_type=F32) + ba_ref[d:d + 1, :])
    i = _sigmoid(jnp.dot(xb, wx_ref[d], preferred_element_type=F32) + bx_ref[d:d + 1, :])
    log_a = (-LRU_C) * r * _softplus(-lam_ref[d:d + 1, :])
    a = jnp.exp(log_a)
    th = jnp.tanh(log_a)
    return a, jnp.sqrt(-2.0 * th / (1.0 - th)) * (i * xc)


def _lru_scan_chunk(a, b, h, rev):
    rows, c = a.shape
    nt = rows // SUBLANES
    a3 = a.reshape(nt, SUBLANES, c)
    b3 = b.reshape(nt, SUBLANES, c)
    ridx = lax.broadcasted_iota(jnp.int32, (nt, SUBLANES, c), 1)
    for s in (1, 2, 4):
        shift = (SUBLANES - s) if rev else s
        a_s = pltpu.roll(a3, shift, 1)
        b_s = pltpu.roll(b3, shift, 1)
        m = (ridx < SUBLANES - s) if rev else (ridx >= s)
        b3 = jnp.where(m, a3 * b_s + b3, b3)
        a3 = jnp.where(m, a3 * a_s, a3)
    out = [None] * nt
    order = range(nt - 1, -1, -1) if rev else range(nt)
    for t in order:
        ht = b3[t] + a3[t] * h
        out[t] = ht
        h = ht[0:1, :] if rev else ht[SUBLANES - 1:SUBLANES, :]
    return jnp.concatenate(out, axis=0), h


def _lru_kernel(ul_ref, gl_ref, uc_ref, cw_ref, cb_ref, wa_ref, ba_ref, wx_ref, bx_ref, lam_ref,
                o_ref, xpl_ref, xpc_ref, xc_ref, hf_ref):
    t_lat = ul_ref.shape[0]
    t_ctx = uc_ref.shape[0]
    c = ul_ref.shape[1]
    rl = min(LRU_ROWS, t_lat)
    rc = min(LRU_ROWS, t_ctx)
    _fill_padded(xpl_ref, ul_ref, t_lat, LRU_PAD)
    _fill_padded(xpc_ref, uc_ref, t_ctx, LRU_PAD)
    cb = cb_ref[...]

    for d in range(2):
        rev = d == 1
        prm = (wa_ref, ba_ref, wx_ref, bx_ref, lam_ref)

        def ctx_body(i, h, rev=rev, d=d, prm=prm):
            blk = (t_ctx // rc - 1 - i) if rev else i
            t0 = pl.multiple_of(blk * rc, rc)
            xc = _conv_rows(xpc_ref, cw_ref, t0, rc, LRU_PAD, 1) + cb
            a, b = _lru_coeffs(xc, d, *prm)
            _, h = _lru_scan_chunk(a, b, h, rev)
            return h

        h = lax.fori_loop(0, t_ctx // rc, ctx_body, jnp.zeros((1, c), F32))

        def lat_body(i, h, rev=rev, d=d, prm=prm):
            blk = (t_lat // rl - 1 - i) if rev else i
            t0 = pl.multiple_of(blk * rl, rl)
            rows = pl.ds(t0, rl)
            if rev:
                xc = xc_ref[rows, :]
            else:
                xc = _conv_rows(xpl_ref, cw_ref, t0, rl, LRU_PAD, 1) + cb
                xc_ref[rows, :] = xc
            a, b = _lru_coeffs(xc, d, *prm)
            hs, h = _lru_scan_chunk(a, b, h, rev)
            if rev:
                y = (hf_ref[rows, :] + hs) * jax.nn.gelu(gl_ref[rows, :], approximate=True)
                o_ref[rows, :] = y.astype(BF16)
            else:
                hf_ref[rows, :] = hs
            return h

        lax.fori_loop(0, t_lat // rl, lat_body, h)


def _lru(p_lat, p_ctx, conv_w, conv_b, w_a, b_a, w_x, b_x, lam, lw):
    b, t, _ = p_lat.shape
    tc = p_ctx.shape[1]
    nb, bd = w_a.shape[1], w_a.shape[2]
    assert bd == LANES and nb * bd == lw
    vec = lambda bb, n: (0, n)
    return pl.pallas_call(
        _lru_kernel,
        out_shape=jax.ShapeDtypeStruct((b, t, lw), BF16),
        grid=(b, nb),
        in_specs=[pl.BlockSpec((None, t, bd), lambda bb, n: (bb, 0, n)),
                  pl.BlockSpec((None, t, bd), lambda bb, n: (bb, 0, nb + n)),
                  pl.BlockSpec((None, tc, bd), lambda bb, n: (bb, 0, n)),
                  pl.BlockSpec((CONV_W, bd), vec),
                  pl.BlockSpec((1, bd), vec),
                  pl.BlockSpec((2, None, bd, bd), lambda bb, n: (0, n, 0, 0)),
                  pl.BlockSpec((2, bd), vec),
                  pl.BlockSpec((2, None, bd, bd), lambda bb, n: (0, n, 0, 0)),
                  pl.BlockSpec((2, bd), vec),
                  pl.BlockSpec((2, bd), vec)],
        out_specs=pl.BlockSpec((None, t, bd), lambda bb, n: (bb, 0, n)),
        scratch_shapes=[pltpu.VMEM((t + 2 * LRU_PAD, bd), F32),
                        pltpu.VMEM((tc + 2 * LRU_PAD, bd), F32),
                        pltpu.VMEM((t, bd), F32),
                        pltpu.VMEM((t, bd), F32)],
        compiler_params=_cparams(("parallel", "parallel")),
        name="lru",
    )(p_lat, p_lat, p_ctx, conv_w, conv_b, w_a, b_a, w_x, b_x, lam)


def _gdnpre_kernel(stride, n_heads, x_ref, cw_ref, o_ref, xp_ref, y_ref):
    t = x_ref.shape[0]
    pad = max(2 * stride, SUBLANES)
    _fill_padded(xp_ref, x_ref, t, pad)
    step = min(t, 512)
    pitch = _row_pitch(stride)
    per = step // stride if stride > 1 else 1

    def conv_body(i, carry):
        t0 = pl.multiple_of(i * step, step)
        y = _conv_rows(xp_ref, cw_ref, t0, step, pad, stride)
        if stride == 1:
            y_ref[pl.ds(t0, step), :] = y
        else:
            for j in range(per):
                dst = pl.multiple_of((i * per + j) * pitch, SUBLANES)
                y_ref[pl.ds(dst, stride), :] = y[j * stride:(j + 1) * stride, :]
        return carry

    lax.fori_loop(0, t // step, conv_body, 0)

    if stride > 1:
        w = stride
        n_rows = t // w
        last = t - w
        ylast = (n_rows - 1) * pitch
        col = lax.broadcasted_iota(jnp.int32, (w, 1), 0)
        not_first = col >= 1
        not_last = col <= w - 2
        prev_bot = xp_ref[pl.ds(pad + last - 1, w), :]
        prev_bot2 = xp_ref[pl.ds(pad + last - w - 1, w), :]
        next_top = xp_ref[pl.ds(pad + 1, w), :]
        y_ref[0:w, :] = y_ref[0:w, :] + jnp.where(
            not_first, cw_ref[1:2, :] * prev_bot + cw_ref[0:1, :] * prev_bot2, 0.0)
        y_ref[pitch:pitch + w, :] = y_ref[pitch:pitch + w, :] + jnp.where(
            not_first, cw_ref[0:1, :] * prev_bot, 0.0)
        y_ref[ylast:ylast + w, :] = y_ref[ylast:ylast + w, :] + jnp.where(
            not_last, cw_ref[3:4, :] * next_top, 0.0)

    blk = pl.program_id(1)
    hd = x_ref.shape[1]

    def finish(normed, scale):
        def act(y):
            y = _silu(y)
            if normed:
                y = y * (lax.rsqrt(jnp.sum(y * y, axis=-1, keepdims=True) + EPS) * scale)
            return y.astype(BF16)

        if stride == 1:
            def body(i, carry):
                rows = pl.ds(pl.multiple_of(i * step, step), step)
                o_ref[rows, :] = act(y_ref[rows, :])
                return carry

            lax.fori_loop(0, t // step, body, 0)
        else:
            n_rows = t // stride

            def body(w, carry):
                dst = pl.ds(pl.multiple_of(w * n_rows, n_rows), n_rows)
                o_ref[dst, :] = act(y_ref[pl.ds(w, n_rows, stride=pitch), :])
                return carry

            lax.fori_loop(0, stride, body, 0, unroll=8)

    @pl.when(blk < n_heads)
    def _():
        finish(True, hd ** -0.5)

    @pl.when(jnp.logical_and(blk >= n_heads, blk < 2 * n_heads))
    def _():
        finish(True, 1.0)

    @pl.when(blk >= 2 * n_heads)
    def _():
        finish(False, 1.0)


def _gdnpre(p, col0, conv_w, n_heads, stride):
    b, t, _ = p.shape
    hd = LANES
    nblk = 3 * n_heads
    assert col0 % hd == 0 and t % stride == 0
    pad = max(2 * stride, SUBLANES)
    y_rows = (t // stride) * _row_pitch(stride)
    return pl.pallas_call(
        functools.partial(_gdnpre_kernel, stride, n_heads),
        out_shape=jax.ShapeDtypeStruct((b, t, nblk * hd), BF16),
        grid=(b, nblk),
        in_specs=[pl.BlockSpec((None, t, hd), lambda bb, n: (bb, 0, col0 // hd + n)),
                  pl.BlockSpec((CONV_W, hd), lambda bb, n: (0, n))],
        out_specs=pl.BlockSpec((None, t, hd), lambda bb, n: (bb, 0, n)),
        scratch_shapes=[pltpu.VMEM((t + 2 * pad, hd), F32), pltpu.VMEM((y_rows, hd), F32)],
        compiler_params=_cparams(("parallel", "parallel")),
        name="gdnpre",
    )(p, conv_w)


GDN_GROUP_ITEMS = 32


def _split3(x):
    hi = x.astype(BF16)
    r1 = x - hi.astype(F32)
    mid = r1.astype(BF16)
    lo = (r1 - mid.astype(F32)).astype(BF16)
    return hi, mid, lo


def _gdn_kernel(mode, rev, d, gh, *refs):
    if mode == "ctx":
        q_ref, k_ref, v_ref, ba_ref, prm_ref, sfin_ref, s_scr = refs
    elif mode == "fwd":
        q_ref, k_ref, v_ref, ba_ref, prm_ref, s0_ref, o_ref, s_scr = refs
    else:
        q_ref, k_ref, v_ref, ba_ref, prm_ref, s0_ref, of_ref, o_ref, s_scr = refs
    r = q_ref.shape[0]
    hd = LANES
    nch = r // CHUNK
    step = pl.program_id(1)
    nsteps = pl.num_programs(1)

    @pl.when(step == 0)
    def _():
        if mode == "ctx":
            s_scr[...] = jnp.zeros(s_scr.shape, F32)
        else:
            s_scr[...] = s0_ref[...]

    ba = ba_ref[...]
    sig = _sigmoid(ba)
    gfull = -jnp.exp(prm_ref[0:1, :]) * _softplus(ba + prm_ref[1:2, :])
    ii = lax.broadcasted_iota(jnp.int32, (r, r), 0)
    jj = lax.broadcasted_iota(jnp.int32, (r, r), 1)
    same = (ii // CHUNK) == (jj // CHUNK)
    tri = jnp.where(jnp.logical_and(same, (jj >= ii) if rev else (jj <= ii)), 1.0, 0.0).astype(BF16)
    g_hi, g_mid, g_lo = _split3(gfull)
    gc = (jnp.dot(tri, g_hi, preferred_element_type=F32)
          + jnp.dot(tri, g_mid, preferred_element_type=F32)
          + jnp.dot(tri, g_lo, preferred_element_type=F32))
    sig_t = sig.T
    gc_t = gc.T
    egc = jnp.exp(gc)

    ci = lax.broadcasted_iota(jnp.int32, (CHUNK, CHUNK), 0)
    cj = lax.broadcasted_iota(jnp.int32, (CHUNK, CHUNK), 1)
    low = (ci <= cj) if rev else (ci >= cj)
    strict = (ci < cj) if rev else (ci > cj)
    eye = jnp.where(ci == cj, 1.0, 0.0).astype(F32)
    last = 0 if rev else CHUNK - 1
    chunk_order = list(range(nch - 1, -1, -1) if rev else range(nch))
    heads_per_group = max(1, min(gh, GDN_GROUP_ITEMS // nch))

    for g0 in range(0, gh, heads_per_group):
        heads = list(range(g0, min(gh, g0 + heads_per_group)))
        items = [(h, c) for h in heads for c in range(nch)]
        st = {}
        for (h, c) in items:
            rows = slice(c * CHUNK, (c + 1) * CHUNK)
            lanes = slice(h * hd, (h + 1) * hd)
            cb = d * gh + h
            cg = 2 * gh + d * gh + h
            bcol = sig[rows, cb:cb + 1]
            gcol = gc[rows, cg:cg + 1]
            brow = sig_t[cb:cb + 1, rows]
            grow = gc_t[cg:cg + 1, rows]
            kc = k_ref[rows, lanes]
            decay = jnp.where(low, jnp.exp(jnp.where(low, gcol - grow, 0.0)), 0.0)
            x = -jnp.where(strict, _mm_nt(kc, kc) * bcol * decay, 0.0)
            it = dict(rows=rows, lanes=lanes, kc=kc, gcol=gcol, brow=brow, grow=grow,
                      egcol=egc[rows, cg:cg + 1], xb=x.astype(BF16), t=eye + x)
            if mode != "ctx":
                it["intra"] = (_mm_nt(q_ref[rows, lanes], kc) * decay).astype(BF16)
            st[(h, c)] = it
        for _ in range(5):
            for it in st.values():
                it["xb"] = jnp.dot(it["xb"], it["xb"], preferred_element_type=F32).astype(BF16)
            for it in st.values():
                it["t"] = it["t"] + jnp.dot(it["t"].astype(BF16), it["xb"],
                                            preferred_element_type=F32)
        for it in st.values():
            tb = it["t"] * it["brow"]
            it["u"] = _mm(tb, v_ref[it["rows"], it["lanes"]])
            it["w"] = _mm(tb * jnp.exp(it["grow"]), it["kc"]).astype(BF16)
            del it["t"], it["xb"]
        state = {h: s_scr[h] for h in heads}
        for c in chunk_order:
            cur = [st[(h, c)] for h in heads]
            sb = [state[h].astype(BF16) for h in heads]
            ws = [jnp.dot(it["w"], s, preferred_element_type=F32) for it, s in zip(cur, sb)]
            if mode != "ctx":
                qs = [jnp.dot(q_ref[it["rows"], it["lanes"]], s, preferred_element_type=F32)
                      for it, s in zip(cur, sb)]
            vnew = [it["u"] - x for it, x in zip(cur, ws)]
            glast = [it["gcol"][last:last + 1, :] for it in cur]
            if mode != "ctx":
                iv = [_mm(it["intra"], v) for it, v in zip(cur, vnew)]
            kv = [_mm_tn(it["kc"], v * jnp.exp(gl - it["gcol"]))
                  for it, v, gl in zip(cur, vnew, glast)]
            for i, h in enumerate(heads):
                it = cur[i]
                if mode != "ctx":
                    o = it["egcol"] * qs[i] + iv[i]
                    if mode == "bwd":
                        o = o + of_ref[it["rows"], it["lanes"]]
                    o_ref[it["rows"], it["lanes"]] = o
                state[h] = state[h] * jnp.exp(glast[i]) + kv[i]
        for h in heads:
            s_scr[h] = state[h]

    if mode == "ctx":
        @pl.when(step == nsteps - 1)
        def _():
            sfin_ref[...] = s_scr[...]


def _gdn_call(mode, rev, d, gh, qkv, ba, prm, r, s0=None, o_f=None):
    b, t, _ = qkv.shape
    hd = LANES
    gw = gh * hd
    n_blk = t // r
    assert t % r == 0 and r % CHUNK == 0 and r % LANES == 0
    blk = (lambda w: n_blk - 1 - w) if rev else (lambda w: w)
    in_specs = [pl.BlockSpec((None, r, gw), lambda bb, w: (bb, blk(w), 0)),
                pl.BlockSpec((None, r, gw), lambda bb, w: (bb, blk(w), 1)),
                pl.BlockSpec((None, r, gw), lambda bb, w: (bb, blk(w), 2)),
                pl.BlockSpec((None, r, LANES), lambda bb, w: (bb, blk(w), 0)),
                pl.BlockSpec((SUBLANES, LANES), lambda bb, w: (0, 0))]
    args = [qkv, qkv, qkv, ba, prm]
    state_spec = pl.BlockSpec((None, gh, hd, hd), lambda bb, w: (bb, 0, 0, 0))
    o_spec = pl.BlockSpec((None, r, gw), lambda bb, w: (bb, blk(w), 0))
    if mode != "ctx":
        in_specs.append(state_spec)
        args.append(s0)
    if mode == "bwd":
        in_specs.append(o_spec)
        args.append(o_f)
    if mode == "ctx":
        out_shape = jax.ShapeDtypeStruct((b, gh, hd, hd), F32)
        out_specs = state_spec
    else:
        out_shape = jax.ShapeDtypeStruct((b, t, gw), F32)
        out_specs = o_spec
    return pl.pallas_call(
        functools.partial(_gdn_kernel, mode, rev, d, gh),
        out_shape=out_shape,
        grid=(b, n_blk),
        in_specs=in_specs,
        out_specs=out_specs,
        scratch_shapes=[pltpu.VMEM((gh, hd, hd), F32)],
        compiler_params=_cparams(("parallel", "arbitrary")),
        name="gdn_%s_%d" % (mode, d),
    )(*args)


def _to_visit_kernel(width, x_ref, o_ref):
    n_rows = x_ref.shape[0] // width

    def body(w, carry):
        dst = pl.ds(pl.multiple_of(w * n_rows, n_rows), n_rows)
        o_ref[dst, :] = x_ref[pl.ds(w, n_rows, stride=width), :]
        return carry

    lax.fori_loop(0, width, body, 0)


def _to_visit(x, width):
    b, t, c = x.shape
    assert c == LANES
    spec = pl.BlockSpec((None, t, c), lambda bb: (bb, 0, 0))
    return pl.pallas_call(
        functools.partial(_to_visit_kernel, width),
        out_shape=jax.ShapeDtypeStruct((b, t, c), F32),
        grid=(b,),
        in_specs=[spec],
        out_specs=spec,
        compiler_params=_cparams(("parallel",)),
        name="to_visit",
    )(x)


def _gdnpost_kernel(width, o_ref, z_ref, ng_ref, y_ref, scr_ref):
    t = o_ref.shape[0]
    n_rows = t // width
    pitch = _row_pitch(width)

    def scatter(w, carry):
        src = pl.ds(pl.multiple_of(w * n_rows, n_rows), n_rows)
        scr_ref[pl.ds(w, n_rows, stride=pitch), :] = o_ref[src, :]
        return carry

    lax.fori_loop(0, width, scatter, 0)
    per = SUBLANES
    ng = ng_ref[...]

    def body(i, carry):
        parts = [scr_ref[pl.ds(pl.multiple_of((i * per + j) * pitch, SUBLANES), width), :]
                 for j in range(per)]
        o = jnp.concatenate(parts, axis=0)
        rows = pl.ds(pl.multiple_of(i * per * width, per * width), per * width)
        y = o * lax.rsqrt(jnp.mean(o * o, axis=-1, keepdims=True) + EPS) * ng
        y_ref[rows, :] = (y * _silu(z_ref[rows, :])).astype(BF16)
        return carry

    lax.fori_loop(0, n_rows // per, body, 0)


def _gdnpost(o, p, zcol0, ng, gh, width):
    b, t, gw = o.shape
    hd = LANES
    n_rows = t // width
    assert zcol0 % hd == 0 and n_rows % SUBLANES == 0
    return pl.pallas_call(
        functools.partial(_gdnpost_kernel, width),
        out_shape=jax.ShapeDtypeStruct((b, t, gw), BF16),
        grid=(b, gh),
        in_specs=[pl.BlockSpec((None, t, hd), lambda bb, h: (bb, 0, h)),
                  pl.BlockSpec((None, t, hd), lambda bb, h: (bb, 0, zcol0 // hd + h)),
                  pl.BlockSpec((1, hd), lambda bb, h: (0, 0))],
        out_specs=pl.BlockSpec((None, t, hd), lambda bb, h: (bb, 0, h)),
        scratch_shapes=[pltpu.VMEM((n_rows * _row_pitch(width), hd), F32)],
        compiler_params=_cparams(("parallel", "parallel")),
        name="gdnpost",
    )(o, p, ng)


def _outproj_kernel(gl_ref, yg_ref, ng_ref, w_ref, o_ref, a_scr):
    tm, lw = gl_ref.shape

    @pl.when(pl.program_id(1) == 0)
    def _():
        ng = ng_ref[...]

        def body(r, carry):
            rows = pl.ds(pl.multiple_of(r * ROW_CHUNK, ROW_CHUNK), ROW_CHUNK)
            x = gl_ref[rows, :].astype(F32)
            y = x * lax.rsqrt(jnp.mean(x * x, axis=-1, keepdims=True) + EPS) * ng
            a_scr[rows, 0:lw] = y.astype(BF16)
            return carry

        lax.fori_loop(0, tm // ROW_CHUNK, body, 0)
        a_scr[:, lw:] = yg_ref[...]

    o_ref[...] = jnp.dot(a_scr[...], w_ref[...], preferred_element_type=F32)


def _outproj(g_lru, y_gdn, ng, w_out):
    m, lw = g_lru.shape
    gw = y_gdn.shape[1]
    k, n = w_out.shape
    tm = _tile(m, 1024, SUBLANES)
    tn = _tile(n, 512)
    return pl.pallas_call(
        _outproj_kernel,
        out_shape=jax.ShapeDtypeStruct((m, n), F32),
        grid=(m // tm, n // tn),
        in_specs=[pl.BlockSpec((tm, lw), lambda i, j: (i, 0)),
                  pl.BlockSpec((tm, gw), lambda i, j: (i, 0)),
                  pl.BlockSpec((1, lw), lambda i, j: (0, 0)),
                  pl.BlockSpec((k, tn), lambda i, j: (0, j))],
        out_specs=pl.BlockSpec((tm, tn), lambda i, j: (i, j)),
        scratch_shapes=[pltpu.VMEM((tm, k), BF16)],
        compiler_params=_cparams(("parallel", "arbitrary")),
        name="outproj",
    )(g_lru, y_gdn, ng, w_out)


def _post_kernel(with_next, *refs):
    if with_next:
        x_ref, y_ref, gp_ref, gate_ref, gn_ref, sh_ref, sc_ref, xo_ref, ho_ref = refs
    else:
        x_ref, y_ref, gp_ref, gate_ref, xo_ref = refs
    y = y_ref[...]
    yn = y * lax.rsqrt(jnp.mean(y * y, axis=-1, keepdims=True) + EPS) * gp_ref[...]
    x1 = x_ref[...] + gate_ref[...] * yn
    xo_ref[...] = x1
    if with_next:
        ho_ref[...] = _norm_mod_rows(x1, gn_ref[...], sc_ref[...], sh_ref[...]).astype(BF16)


def _post(x, y, g_post, gate, nxt=None):
    b, t, d = x.shape
    tr = _tile(t, 256, SUBLANES)
    row = pl.BlockSpec((None, tr, d), lambda bb, i: (bb, i, 0))
    vec = pl.BlockSpec((1, d), lambda bb, i: (0, 0))
    mod = pl.BlockSpec((None, 1, d), lambda bb, i: (bb, 0, 0))
    in_specs = [row, row, vec, mod]
    args = [x, y, g_post, gate]
    out_shape = [jax.ShapeDtypeStruct((b, t, d), F32)]
    out_specs = [row]
    if nxt is not None:
        g_next, sh, sc = nxt
        in_specs += [vec, mod, mod]
        args += [g_next, sh, sc]
        out_shape.append(jax.ShapeDtypeStruct((b, t, d), BF16))
        out_specs.append(row)
    return pl.pallas_call(
        functools.partial(_post_kernel, nxt is not None),
        out_shape=tuple(out_shape),
        grid=(b, t // tr),
        in_specs=in_specs,
        out_specs=tuple(out_specs),
        compiler_params=_cparams(("parallel", "parallel")),
        name="post_next" if nxt is not None else "post",
    )(*args)


def _ffn_kernel(h_ref, wg_ref, wu_ref, wd_ref, o_ref):
    @pl.when(pl.program_id(1) == 0)
    def _():
        o_ref[...] = jnp.zeros(o_ref.shape, F32)

    h = h_ref[...]
    g = jnp.dot(h, wg_ref[...], preferred_element_type=F32)
    u = jnp.dot(h, wu_ref[...], preferred_element_type=F32)
    o_ref[...] += jnp.dot((_silu(g) * u).astype(BF16), wd_ref[...], preferred_element_type=F32)


def _ffn(h, wg, wu, wd):
    m, d = h.shape
    dff = wg.shape[1]
    tm = _tile(m, 512, SUBLANES)
    tf = _tile(dff, 512)
    return pl.pallas_call(
        _ffn_kernel,
        out_shape=jax.ShapeDtypeStruct((m, d), F32),
        grid=(m // tm, dff // tf),
        in_specs=[pl.BlockSpec((tm, d), lambda i, f: (i, 0)),
                  pl.BlockSpec((d, tf), lambda i, f: (0, f)),
                  pl.BlockSpec((d, tf), lambda i, f: (0, f)),
                  pl.BlockSpec((tf, d), lambda i, f: (f, 0))],
        out_specs=pl.BlockSpec((tm, d), lambda i, f: (i, 0)),
        compiler_params=_cparams(("parallel", "arbitrary")),
        name="ffn",
    )(h, wg, wu, wd)


def _pad_cols(w, mult):
    n = w.shape[-1]
    padn = (-n) % mult
    return w if padn == 0 else jnp.pad(w, ((0, 0), (0, padn)))


def kernel(x, c, ctx, c_ctx, w_ada, b_ada, g_pre_mix, g_post_mix, g_pre_ffn, g_post_ffn, w_in,
           lru_conv_w, lru_conv_b, lru_w_a, lru_b_a, lru_w_x, lru_b_x, lru_lambda, lru_norm_g,
           gdn_conv_w, gdn_a_log, gdn_dt_bias, gdn_norm_g, w_out, w_ffn_gate, w_ffn_up, w_ffn_down):
    depth = w_ada.shape[0]
    bsz, seq, d = x.shape
    tctx = ctx.shape[1]
    lw = lru_lambda.shape[-1]
    gh = gdn_a_log.shape[-1]
    hd = gdn_norm_g.shape[-1]
    gw = gh * hd
    n_main = 2 * lw + 4 * gw
    rows = seq // GRID_W
    assert hd == LANES and 4 * gh <= LANES and seq == rows * GRID_W
    assert n_main % gw == 0 and (2 * lw + 3 * gw) % gw == 0

    for l in range(depth):
        assert l == depth - 1, "context stream updates (depth > 1) are not implemented"

        crows = jnp.zeros((SUBLANES, d), F32).at[0:bsz].set(c).at[bsz].set(c_ctx)
        mod = _ada(crows, w_ada[l], b_ada[l][None, :])
        mod_lat = mod[0:bsz].reshape(bsz, N_MOD, 1, d)
        shift_m, scale_m, gate_m, shift_f, scale_f, gate_f = (mod_lat[:, i] for i in range(N_MOD))
        mod_ctx = mod[bsz].reshape(N_MOD, 1, 1, d)

        w_all = w_in[l].astype(BF16)
        w_ba = _pad_cols(w_in[l][:, n_main:], LANES).astype(BF16)
        g_pre = g_pre_mix[l][None, :]
        p_lat, ba_lat = _inproj(_prenorm(x, g_pre, shift_m, scale_m), w_all, n_main, w_ba)
        p_ctx, ba_ctx = _inproj(_prenorm(ctx, g_pre, mod_ctx[0], mod_ctx[1]), w_all, n_main, w_ba)

        g_lru = _lru(p_lat, p_ctx, lru_conv_w[l], lru_conv_b[l][None, :], lru_w_a[l].astype(BF16),
                     lru_b_a[l], lru_w_x[l].astype(BF16), lru_b_x[l], lru_lambda[l], lw)

        qkv_lat = _gdnpre(p_lat, 2 * lw, gdn_conv_w[l], gh, GRID_W)
        qkv_ctx = _gdnpre(p_ctx, 2 * lw, gdn_conv_w[l], gh, 1)
        prm = jnp.zeros((SUBLANES, LANES), F32)
        prm = prm.at[0, 2 * gh:4 * gh].set(gdn_a_log[l].reshape(-1))
        prm = prm.at[1, 2 * gh:4 * gh].set(gdn_dt_bias[l].reshape(-1))
        ba_vis = _to_visit(ba_lat, GRID_W)
        s_f = _gdn_call("ctx", False, 0, gh, qkv_ctx, ba_ctx, prm, tctx)
        s_b = _gdn_call("ctx", True, 1, gh, qkv_ctx, ba_ctx, prm, tctx)
        o_f = _gdn_call("fwd", False, 0, gh, qkv_lat, ba_vis, prm, rows, s0=s_f)
        o_fb = _gdn_call("bwd", True, 1, gh, qkv_lat, ba_vis, prm, rows, s0=s_b, o_f=o_f)
        y_gdn = _gdnpost(o_fb, p_lat, 2 * lw + 3 * gw, gdn_norm_g[l][None, :], gh, GRID_W)
        y_gdn = y_gdn.reshape(bsz * seq, gw)

        mix = _outproj(g_lru.reshape(bsz * seq, lw), y_gdn, lru_norm_g[l][None, :],
                       w_out[l].astype(BF16))
        x1, h_ffn = _post(x, mix.reshape(bsz, seq, d), g_post_mix[l][None, :], gate_m,
                          nxt=(g_pre_ffn[l][None, :], shift_f, scale_f))
        dff_mult = 512
        wg = _pad_cols(w_ffn_gate[l], dff_mult).astype(BF16)
        wu = _pad_cols(w_ffn_up[l], dff_mult).astype(BF16)
        wd = jnp.pad(w_ffn_down[l], ((0, wg.shape[1] - w_ffn_down[l].shape[0]), (0, 0))).astype(BF16)
        y_ffn = _ffn(h_ffn.reshape(bsz * seq, d), wg, wu, wd)
        x = _post(x1, y_ffn.reshape(bsz, seq, d), g_post_ffn[l][None, :], gate_f)[0]
    return x
```

```python
import functools

import jax
import jax.numpy as jnp
from jax import lax
from jax.experimental import pallas as pl
from jax.experimental.pallas import tpu as pltpu

F32 = jnp.float32
BF16 = jnp.bfloat16

EPS = 1e-6
GRID_W = 64
CHUNK = 64
LRU_C = 8.0
CONV_W = 4
N_MOD = 6
LANES = 128
SUBLANES = 8
VMEM_LIMIT = 56 * 1024 * 1024


def _cparams(sem):
    return pltpu.CompilerParams(dimension_semantics=sem, vmem_limit_bytes=VMEM_LIMIT)


def _mm(a, b):
    return jnp.dot(a.astype(BF16), b.astype(BF16), preferred_element_type=F32)


def _mm_nt(a, b):
    return lax.dot_general(a.astype(BF16), b.astype(BF16), (((1,), (1,)), ((), ())),
                           preferred_element_type=F32)


def _mm_tn(a, b):
    return lax.dot_general(a.astype(BF16), b.astype(BF16), (((0,), (0,)), ((), ())),
                           preferred_element_type=F32)


def _sigmoid(x):
    return 1.0 / (1.0 + jnp.exp(-x))


def _silu(x):
    return x * _sigmoid(x)


def _softplus(x):
    return jnp.maximum(x, 0.0) + jnp.log1p(jnp.exp(-jnp.abs(x)))


def _tile(n, target, mult=LANES):
    if n <= target:
        return n
    t = (target // mult) * mult
    while t >= mult:
        if n % t == 0:
            return t
        t -= mult
    return n


def _ada_kernel(c_ref, w_ref, b_ref, o_ref):
    o_ref[...] = _mm(_silu(c_ref[...]), w_ref[...]) + b_ref[...]


def _ada(crows, w, b):
    r, d = crows.shape
    n = w.shape[1]
    tn = _tile(n, 512)
    return pl.pallas_call(
        _ada_kernel,
        out_shape=jax.ShapeDtypeStruct((r, n), F32),
        grid=(n // tn,),
        in_specs=[pl.BlockSpec((r, d), lambda j: (0, 0)),
                  pl.BlockSpec((d, tn), lambda j: (0, j)),
                  pl.BlockSpec((1, tn), lambda j: (0, j))],
        out_specs=pl.BlockSpec((r, tn), lambda j: (0, j)),
        compiler_params=_cparams(("arbitrary",)),
        name="ada",
    )(crows, w, b)


ROW_CHUNK = 64


def _norm_mod_rows(x, g, sc, sh):
    y = x * lax.rsqrt(jnp.mean(x * x, axis=-1, keepdims=True) + EPS) * g
    return y * (1.0 + sc) + sh


def _prenorm_kernel(x_ref, g_ref, sh_ref, sc_ref, o_ref):
    o_ref[...] = _norm_mod_rows(x_ref[...], g_ref[...], sc_ref[...], sh_ref[...]).astype(BF16)


def _prenorm(x, g, sh, sc):
    b, t, d = x.shape
    tr = _tile(t, 256, SUBLANES)
    per_batch = sh.shape[0] > 1
    mod_map = (lambda bb, i: (bb, 0, 0)) if per_batch else (lambda bb, i: (0, 0, 0))
    row = pl.BlockSpec((None, tr, d), lambda bb, i: (bb, i, 0))
    return pl.pallas_call(
        _prenorm_kernel,
        out_shape=jax.ShapeDtypeStruct((b, t, d), BF16),
        grid=(b, t // tr),
        in_specs=[row, pl.BlockSpec((1, d), lambda bb, i: (0, 0)),
                  pl.BlockSpec((None, 1, d), mod_map), pl.BlockSpec((None, 1, d), mod_map)],
        out_specs=row,
        compiler_params=_cparams(("parallel", "parallel")),
        name="prenorm",
    )(x, g, sh, sc)


def _inproj_kernel(h_ref, w_ref, wba_ref, o_ref, oba_ref):
    @pl.when(pl.program_id(2) == 0)
    def _():
        oba_ref[...] = jnp.dot(h_ref[...], wba_ref[...], preferred_element_type=F32)

    o_ref[...] = jnp.dot(h_ref[...], w_ref[...], preferred_element_type=F32)


def _inproj(h, w, n, w_ba):
    b, t, d = h.shape
    nba = w_ba.shape[1]
    tm = _tile(t, 1024, SUBLANES)
    tn = _tile(n, 768)
    return pl.pallas_call(
        _inproj_kernel,
        out_shape=(jax.ShapeDtypeStruct((b, t, n), F32), jax.ShapeDtypeStruct((b, t, nba), F32)),
        grid=(b, t // tm, n // tn),
        in_specs=[pl.BlockSpec((None, tm, d), lambda bb, i, j: (bb, i, 0)),
                  pl.BlockSpec((d, tn), lambda bb, i, j: (0, j)),
                  pl.BlockSpec((d, nba), lambda bb, i, j: (0, 0))],
        out_specs=(pl.BlockSpec((None, tm, tn), lambda bb, i, j: (bb, i, j)),
                   pl.BlockSpec((None, tm, nba), lambda bb, i, j: (bb, i, 0))),
        compiler_params=_cparams(("parallel", "parallel", "arbitrary")),
        name="inproj",
    )(h, w, w_ba)


def _fill_padded(xp_ref, x_ref, n, pad):
    zeros = jnp.zeros((pad, xp_ref.shape[1]), F32)
    xp_ref[0:pad, :] = zeros
    xp_ref[pad + n:pad + n + pad, :] = zeros
    step = min(n, 512)

    def body(i, carry):
        rows = pl.ds(pl.multiple_of(i * step, step), step)
        xp_ref[pl.ds(pl.multiple_of(pad + i * step, SUBLANES), step), :] = x_ref[rows, :]
        return carry

    lax.fori_loop(0, n // step, body, 0)


def _row_pitch(stride):
    return stride if stride == 1 else stride + SUBLANES


def _conv_rows(xp_ref, cw_ref, t0, rows, pad, stride):
    acc = None
    for j in range(CONV_W):
        off = pad + (j - 2) * stride
        term = cw_ref[j:j + 1, :] * xp_ref[pl.ds(t0 + off, rows), :]
        acc = term if acc is None else acc + term
    return acc


LRU_ROWS = 512
LRU_PAD = 8


def _lru_coeffs(xc, d, wa_ref, ba_ref, wx_ref, bx_ref, lam_ref):
    xb = xc.astype(BF16)
    r = _sigmoid(jnp.dot(xb, wa_ref[d], preferred_element_type=F32) + ba_ref[d:d + 1, :])
    i = _sigmoid(jnp.dot(xb, wx_ref[d], preferred_element_type=F32) + bx_ref[d:d + 1, :])
    log_a = (-LRU_C) * r * _softplus(-lam_ref[d:d + 1, :])
    a = jnp.exp(log_a)
    th = jnp.tanh(log_a)
    return a, jnp.sqrt(-2.0 * th / (1.0 - th)) * (i * xc)


def _lru_scan_chunk(a, b, h, rev):
    rows, c = a.shape
    nt = rows // SUBLANES
    a3 = a.reshape(nt, SUBLANES, c)
    b3 = b.reshape(nt, SUBLANES, c)
    ridx = lax.broadcasted_iota(jnp.int32, (nt, SUBLANES, c), 1)
    for s in (1, 2, 4):
        shift = (SUBLANES - s) if rev else s
        a_s = pltpu.roll(a3, shift, 1)
        b_s = pltpu.roll(b3, shift, 1)
        m = (ridx < SUBLANES - s) if rev else (ridx >= s)
        b3 = jnp.where(m, a3 * b_s + b3, b3)
        a3 = jnp.where(m, a3 * a_s, a3)
    out = [None] * nt
    order = range(nt - 1, -1, -1) if rev else range(nt)
    for t in order:
        ht = b3[t] + a3[t] * h
        out[t] = ht
        h = ht[0:1, :] if rev else ht[SUBLANES - 1:SUBLANES, :]
    return jnp.concatenate(out, axis=0), h


def _lru_kernel(ul_ref, gl_ref, uc_ref, cw_ref, cb_ref, wa_ref, ba_ref, wx_ref, bx_ref, lam_ref,
                o_ref, xpl_ref, xpc_ref, xc_ref, hf_ref):
    t_lat = ul_ref.shape[0]
    t_ctx = uc_ref.shape[0]
    c = ul_ref.shape[1]
    rl = min(LRU_ROWS, t_lat)
    rc = min(LRU_ROWS, t_ctx)
    _fill_padded(xpl_ref, ul_ref, t_lat, LRU_PAD)
    _fill_padded(xpc_ref, uc_ref, t_ctx, LRU_PAD)
    cb = cb_ref[...]

    for d in range(2):
        rev = d == 1
        prm = (wa_ref, ba_ref, wx_ref, bx_ref, lam_ref)

        def ctx_body(i, h, rev=rev, d=d, prm=prm):
            blk = (t_ctx // rc - 1 - i) if rev else i
            t0 = pl.multiple_of(blk * rc, rc)
            xc = _conv_rows(xpc_ref, cw_ref, t0, rc, LRU_PAD, 1) + cb
            a, b = _lru_coeffs(xc, d, *prm)
            _, h = _lru_scan_chunk(a, b, h, rev)
            return h

        h = lax.fori_loop(0, t_ctx // rc, ctx_body, jnp.zeros((1, c), F32))

        def lat_body(i, h, rev=rev, d=d, prm=prm):
            blk = (t_lat // rl - 1 - i) if rev else i
            t0 = pl.multiple_of(blk * rl, rl)
            rows = pl.ds(t0, rl)
            if rev:
                xc = xc_ref[rows, :]
            else:
                xc = _conv_rows(xpl_ref, cw_ref, t0, rl, LRU_PAD, 1) + cb
                xc_ref[rows, :] = xc
            a, b = _lru_coeffs(xc, d, *prm)
            hs, h = _lru_scan_chunk(a, b, h, rev)
            if rev:
                y = (hf_ref[rows, :] + hs) * jax.nn.gelu(gl_ref[rows, :], approximate=True)
                o_ref[rows, :] = y.astype(BF16)
            else:
                hf_ref[rows, :] = hs
            return h

        lax.fori_loop(0, t_lat // rl, lat_body, h)


def _lru(p_lat, p_ctx, conv_w, conv_b, w_a, b_a, w_x, b_x, lam, lw):
    b, t, _ = p_lat.shape
    tc = p_ctx.shape[1]
    nb, bd = w_a.shape[1], w_a.shape[2]
    assert bd == LANES and nb * bd == lw
    vec = lambda bb, n: (0, n)
    return pl.pallas_call(
        _lru_kernel,
        out_shape=jax.ShapeDtypeStruct((b, t, lw), BF16),
        grid=(b, nb),
        in_specs=[pl.BlockSpec((None, t, bd), lambda bb, n: (bb, 0, n)),
                  pl.BlockSpec((None, t, bd), lambda bb, n: (bb, 0, nb + n)),
                  pl.BlockSpec((None, tc, bd), lambda bb, n: (bb, 0, n)),
                  pl.BlockSpec((CONV_W, bd), vec),
                  pl.BlockSpec((1, bd), vec),
                  pl.BlockSpec((2, None, bd, bd), lambda bb, n: (0, n, 0, 0)),
                  pl.BlockSpec((2, bd), vec),
                  pl.BlockSpec((2, None, bd, bd), lambda bb, n: (0, n, 0, 0)),
                  pl.BlockSpec((2, bd), vec),
                  pl.BlockSpec((2, bd), vec)],
        out_specs=pl.BlockSpec((None, t, bd), lambda bb, n: (bb, 0, n)),
        scratch_shapes=[pltpu.VMEM((t + 2 * LRU_PAD, bd), F32),
                        pltpu.VMEM((tc + 2 * LRU_PAD, bd), F32),
                        pltpu.VMEM((t, bd), F32),
                        pltpu.VMEM((t, bd), F32)],
        compiler_params=_cparams(("parallel", "parallel")),
        name="lru",
    )(p_lat, p_lat, p_ctx, conv_w, conv_b, w_a, b_a, w_x, b_x, lam)


def _gdnpre_kernel(stride, n_heads, x_ref, cw_ref, o_ref, xp_ref, y_ref):
    t = x_ref.shape[0]
    pad = max(2 * stride, SUBLANES)
    _fill_padded(xp_ref, x_ref, t, pad)
    step = min(t, 512)
    pitch = _row_pitch(stride)
    per = step // stride if stride > 1 else 1

    def conv_body(i, carry):
        t0 = pl.multiple_of(i * step, step)
        y = _conv_rows(xp_ref, cw_ref, t0, step, pad, stride)
        if stride == 1:
            y_ref[pl.ds(t0, step), :] = y
        else:
            for j in range(per):
                dst = pl.multiple_of((i * per + j) * pitch, SUBLANES)
                y_ref[pl.ds(dst, stride), :] = y[j * stride:(j + 1) * stride, :]
        return carry

    lax.fori_loop(0, t // step, conv_body, 0)

    if stride > 1:
        w = stride
        n_rows = t // w
        last = t - w
        ylast = (n_rows - 1) * pitch
        col = lax.broadcasted_iota(jnp.int32, (w, 1), 0)
        not_first = col >= 1
        not_last = col <= w - 2
        prev_bot = xp_ref[pl.ds(pad + last - 1, w), :]
        prev_bot2 = xp_ref[pl.ds(pad + last - w - 1, w), :]
        next_top = xp_ref[pl.ds(pad + 1, w), :]
        y_ref[0:w, :] = y_ref[0:w, :] + jnp.where(
            not_first, cw_ref[1:2, :] * prev_bot + cw_ref[0:1, :] * prev_bot2, 0.0)
        y_ref[pitch:pitch + w, :] = y_ref[pitch:pitch + w, :] + jnp.where(
            not_first, cw_ref[0:1, :] * prev_bot, 0.0)
        y_ref[ylast:ylast + w, :] = y_ref[ylast:ylast + w, :] + jnp.where(
            not_last, cw_ref[3:4, :] * next_top, 0.0)

    blk = pl.program_id(1)
    hd = x_ref.shape[1]

    def finish(normed, scale):
        def act(y):
            y = _silu(y)
            if normed:
                y = y * (lax.rsqrt(jnp.sum(y * y, axis=-1, keepdims=True) + EPS) * scale)
            return y.astype(BF16)

        if stride == 1:
            def body(i, carry):
                rows = pl.ds(pl.multiple_of(i * step, step), step)
                o_ref[rows, :] = act(y_ref[rows, :])
                return carry

            lax.fori_loop(0, t // step, body, 0)
        else:
            n_rows = t // stride

            def body(w, carry):
                dst = pl.ds(pl.multiple_of(w * n_rows, n_rows), n_rows)
                o_ref[dst, :] = act(y_ref[pl.ds(w, n_rows, stride=pitch), :])
                return carry

            lax.fori_loop(0, stride, body, 0, unroll=8)

    @pl.when(blk < n_heads)
    def _():
        finish(True, hd ** -0.5)

    @pl.when(jnp.logical_and(blk >= n_heads, blk < 2 * n_heads))
    def _():
        finish(True, 1.0)

    @pl.when(blk >= 2 * n_heads)
    def _():
        finish(False, 1.0)


def _gdnpre(p, col0, conv_w, n_heads, stride):
    b, t, _ = p.shape
    hd = LANES
    nblk = 3 * n_heads
    assert col0 % hd == 0 and t % stride == 0
    pad = max(2 * stride, SUBLANES)
    y_rows = (t // stride) * _row_pitch(stride)
    return pl.pallas_call(
        functools.partial(_gdnpre_kernel, stride, n_heads),
        out_shape=jax.ShapeDtypeStruct((b, t, nblk * hd), BF16),
        grid=(b, nblk),
        in_specs=[pl.BlockSpec((None, t, hd), lambda bb, n: (bb, 0, col0 // hd + n)),
                  pl.BlockSpec((CONV_W, hd), lambda bb, n: (0, n))],
        out_specs=pl.BlockSpec((None, t, hd), lambda bb, n: (bb, 0, n)),
        scratch_shapes=[pltpu.VMEM((t + 2 * pad, hd), F32), pltpu.VMEM((y_rows, hd), F32)],
        compiler_params=_cparams(("parallel", "parallel")),
        name="gdnpre",
    )(p, conv_w)


GDN_GROUP_ITEMS = 32


def _split3(x):
    hi = x.astype(BF16)
    r1 = x - hi.astype(F32)
    mid = r1.astype(BF16)
    lo = (r1 - mid.astype(F32)).astype(BF16)
    return hi, mid, lo


def _gdn_kernel(mode, rev, d, gh, n_cast, *refs):
    n_in = {"ctx": 5, "fwd": 6, "bwd": 7}[mode]
    cast_in = refs[n_in:n_in + n_cast]
    cast_out = refs[n_in + n_cast + 1:n_in + 2 * n_cast + 1]
    refs = refs[:n_in] + refs[n_in + n_cast:n_in + n_cast + 1] + refs[n_in + 2 * n_cast + 1:]
    if mode == "ctx":
        q_ref, k_ref, v_ref, ba_ref, prm_ref, sfin_ref, s_scr = refs
    elif mode == "fwd":
        q_ref, k_ref, v_ref, ba_ref, prm_ref, s0_ref, o_ref, s_scr = refs
    else:
        q_ref, k_ref, v_ref, ba_ref, prm_ref, s0_ref, of_ref, o_ref, s_scr = refs
    for src, dst in zip(cast_in, cast_out):
        dst[...] = src[...].astype(BF16)
    r = q_ref.shape[0]
    hd = LANES
    nch = r // CHUNK
    step = pl.program_id(1)
    nsteps = pl.num_programs(1)

    @pl.when(step == 0)
    def _():
        if mode == "ctx":
            s_scr[...] = jnp.zeros(s_scr.shape, F32)
        else:
            s_scr[...] = s0_ref[...]

    ba = ba_ref[...]
    sig = _sigmoid(ba)
    gfull = -jnp.exp(prm_ref[0:1, :]) * _softplus(ba + prm_ref[1:2, :])
    ii = lax.broadcasted_iota(jnp.int32, (r, r), 0)
    jj = lax.broadcasted_iota(jnp.int32, (r, r), 1)
    same = (ii // CHUNK) == (jj // CHUNK)
    tri = jnp.where(jnp.logical_and(same, (jj >= ii) if rev else (jj <= ii)), 1.0, 0.0).astype(BF16)
    g_hi, g_mid, g_lo = _split3(gfull)
    gc = (jnp.dot(tri, g_hi, preferred_element_type=F32)
          + jnp.dot(tri, g_mid, preferred_element_type=F32)
          + jnp.dot(tri, g_lo, preferred_element_type=F32))
    sig_t = sig.T
    gc_t = gc.T
    egc = jnp.exp(gc)

    ci = lax.broadcasted_iota(jnp.int32, (CHUNK, CHUNK), 0)
    cj = lax.broadcasted_iota(jnp.int32, (CHUNK, CHUNK), 1)
    low = (ci <= cj) if rev else (ci >= cj)
    strict = (ci < cj) if rev else (ci > cj)
    eye = jnp.where(ci == cj, 1.0, 0.0).astype(F32)
    last = 0 if rev else CHUNK - 1
    chunk_order = list(range(nch - 1, -1, -1) if rev else range(nch))
    heads_per_group = max(1, min(gh, GDN_GROUP_ITEMS // nch))

    for g0 in range(0, gh, heads_per_group):
        heads = list(range(g0, min(gh, g0 + heads_per_group)))
        items = [(h, c) for h in heads for c in range(nch)]
        st = {}
        for (h, c) in items:
            rows = slice(c * CHUNK, (c + 1) * CHUNK)
            lanes = slice(h * hd, (h + 1) * hd)
            cb = d * gh + h
            cg = 2 * gh + d * gh + h
            bcol = sig[rows, cb:cb + 1]
            gcol = gc[rows, cg:cg + 1]
            brow = sig_t[cb:cb + 1, rows]
            grow = gc_t[cg:cg + 1, rows]
            kc = k_ref[rows, lanes]
            decay = jnp.where(low, jnp.exp(jnp.where(low, gcol - grow, 0.0)), 0.0)
            x = -jnp.where(strict, _mm_nt(kc, kc) * bcol * decay, 0.0)
            it = dict(rows=rows, lanes=lanes, kc=kc, gcol=gcol, brow=brow, grow=grow,
                      egcol=egc[rows, cg:cg + 1], xb=x.astype(BF16), t=eye + x)
            if mode != "ctx":
                it["intra"] = (_mm_nt(q_ref[rows, lanes], kc) * decay).astype(BF16)
            st[(h, c)] = it
        for _ in range(5):
            for it in st.values():
                it["xb"] = jnp.dot(it["xb"], it["xb"], preferred_element_type=F32).astype(BF16)
            for it in st.values():
                it["t"] = it["t"] + jnp.dot(it["t"].astype(BF16), it["xb"],
                                            preferred_element_type=F32)
        for it in st.values():
            tb = it["t"] * it["brow"]
            it["u"] = _mm(tb, v_ref[it["rows"], it["lanes"]])
            it["w"] = _mm(tb * jnp.exp(it["grow"]), it["kc"]).astype(BF16)
            del it["t"], it["xb"]
        state = {h: s_scr[h] for h in heads}
        for c in chunk_order:
            cur = [st[(h, c)] for h in heads]
            sb = [state[h].astype(BF16) for h in heads]
            ws = [jnp.dot(it["w"], s, preferred_element_type=F32) for it, s in zip(cur, sb)]
            if mode != "ctx":
                qs = [jnp.dot(q_ref[it["rows"], it["lanes"]], s, preferred_element_type=F32)
                      for it, s in zip(cur, sb)]
            vnew = [it["u"] - x for it, x in zip(cur, ws)]
            glast = [it["gcol"][last:last + 1, :] for it in cur]
            if mode != "ctx":
                iv = [_mm(it["intra"], v) for it, v in zip(cur, vnew)]
            kv = [_mm_tn(it["kc"], v * jnp.exp(gl - it["gcol"]))
                  for it, v, gl in zip(cur, vnew, glast)]
            for i, h in enumerate(heads):
                it = cur[i]
                if mode != "ctx":
                    o = it["egcol"] * qs[i] + iv[i]
                    if mode == "bwd":
                        o = o + of_ref[it["rows"], it["lanes"]]
                    o_ref[it["rows"], it["lanes"]] = o
                state[h] = state[h] * jnp.exp(glast[i]) + kv[i]
        for h in heads:
            s_scr[h] = state[h]

    if mode == "ctx":
        @pl.when(step == nsteps - 1)
        def _():
            sfin_ref[...] = s_scr[...]


def _gdn_call(mode, rev, d, gh, qkv, ba, prm, r, s0=None, o_f=None, casts=()):
    b, t, _ = qkv.shape
    hd = LANES
    gw = gh * hd
    n_blk = t // r
    assert t % r == 0 and r % CHUNK == 0 and r % LANES == 0
    blk = (lambda w: n_blk - 1 - w) if rev else (lambda w: w)
    in_specs = [pl.BlockSpec((None, r, gw), lambda bb, w: (bb, blk(w), 0)),
                pl.BlockSpec((None, r, gw), lambda bb, w: (bb, blk(w), 1)),
                pl.BlockSpec((None, r, gw), lambda bb, w: (bb, blk(w), 2)),
                pl.BlockSpec((None, r, LANES), lambda bb, w: (bb, blk(w), 0)),
                pl.BlockSpec((SUBLANES, LANES), lambda bb, w: (0, 0))]
    args = [qkv, qkv, qkv, ba, prm]
    state_spec = pl.BlockSpec((None, gh, hd, hd), lambda bb, w: (bb, 0, 0, 0))
    o_spec = pl.BlockSpec((None, r, gw), lambda bb, w: (bb, blk(w), 0))
    if mode != "ctx":
        in_specs.append(state_spec)
        args.append(s0)
    if mode == "bwd":
        in_specs.append(o_spec)
        args.append(o_f)
    if mode == "ctx":
        out_shape = [jax.ShapeDtypeStruct((b, gh, hd, hd), F32)]
        out_specs = [state_spec]
    else:
        out_shape = [jax.ShapeDtypeStruct((b, t, gw), F32)]
        out_specs = [o_spec]
    n_steps = b * n_blk
    for wmat in casts:
        nr, nc = wmat.shape
        rp = _cast_rows(nr, n_steps)
        last = nr // rp - 1
        spec = pl.BlockSpec((rp, nc), lambda bb, w, last=last: (jnp.minimum(bb * n_blk + w, last), 0))
        in_specs.append(spec)
        args.append(wmat)
        out_shape.append(jax.ShapeDtypeStruct((nr, nc), BF16))
        out_specs.append(spec)
    outs = pl.pallas_call(
        functools.partial(_gdn_kernel, mode, rev, d, gh, len(casts)),
        out_shape=tuple(out_shape),
        grid=(b, n_blk),
        in_specs=in_specs,
        out_specs=tuple(out_specs),
        scratch_shapes=[pltpu.VMEM((gh, hd, hd), F32)],
        compiler_params=_cparams(("arbitrary", "arbitrary")),
        name="gdn_%s_%d" % (mode, d),
    )(*args)
    return outs[0] if not casts else outs


def _cast_rows(n_rows, n_steps):
    for rp in range(2 * SUBLANES, n_rows, 2 * SUBLANES):
        if n_rows % rp == 0 and n_rows // rp <= n_steps:
            return rp
    return n_rows


def _to_visit_kernel(width, x_ref, o_ref):
    n_rows = x_ref.shape[0] // width

    def body(w, carry):
        dst = pl.ds(pl.multiple_of(w * n_rows, n_rows), n_rows)
        o_ref[dst, :] = x_ref[pl.ds(w, n_rows, stride=width), :]
        return carry

    lax.fori_loop(0, width, body, 0)


def _to_visit(x, width):
    b, t, c = x.shape
    assert c == LANES
    spec = pl.BlockSpec((None, t, c), lambda bb: (bb, 0, 0))
    return pl.pallas_call(
        functools.partial(_to_visit_kernel, width),
        out_shape=jax.ShapeDtypeStruct((b, t, c), F32),
        grid=(b,),
        in_specs=[spec],
        out_specs=spec,
        compiler_params=_cparams(("parallel",)),
        name="to_visit",
    )(x)


def _gdnpost_kernel(width, o_ref, z_ref, ng_ref, y_ref, scr_ref):
    t = o_ref.shape[0]
    n_rows = t // width
    pitch = _row_pitch(width)

    def scatter(w, carry):
        src = pl.ds(pl.multiple_of(w * n_rows, n_rows), n_rows)
        scr_ref[pl.ds(w, n_rows, stride=pitch), :] = o_ref[src, :]
        return carry

    lax.fori_loop(0, width, scatter, 0)
    per = SUBLANES
    ng = ng_ref[...]

    def body(i, carry):
        parts = [scr_ref[pl.ds(pl.multiple_of((i * per + j) * pitch, SUBLANES), width), :]
                 for j in range(per)]
        o = jnp.concatenate(parts, axis=0)
        rows = pl.ds(pl.multiple_of(i * per * width, per * width), per * width)
        y = o * lax.rsqrt(jnp.mean(o * o, axis=-1, keepdims=True) + EPS) * ng
        y_ref[rows, :] = (y * _silu(z_ref[rows, :])).astype(BF16)
        return carry

    lax.fori_loop(0, n_rows // per, body, 0)


def _gdnpost(o, p, zcol0, ng, gh, width):
    b, t, gw = o.shape
    hd = LANES
    n_rows = t // width
    assert zcol0 % hd == 0 and n_rows % SUBLANES == 0
    return pl.pallas_call(
        functools.partial(_gdnpost_kernel, width),
        out_shape=jax.ShapeDtypeStruct((b, t, gw), BF16),
        grid=(b, gh),
        in_specs=[pl.BlockSpec((None, t, hd), lambda bb, h: (bb, 0, h)),
                  pl.BlockSpec((None, t, hd), lambda bb, h: (bb, 0, zcol0 // hd + h)),
                  pl.BlockSpec((1, hd), lambda bb, h: (0, 0))],
        out_specs=pl.BlockSpec((None, t, hd), lambda bb, h: (bb, 0, h)),
        scratch_shapes=[pltpu.VMEM((n_rows * _row_pitch(width), hd), F32)],
        compiler_params=_cparams(("parallel", "parallel")),
        name="gdnpost",
    )(o, p, ng)


def _outproj_kernel(gl_ref, yg_ref, ng_ref, w_ref, o_ref, a_scr):
    tm, lw = gl_ref.shape

    @pl.when(pl.program_id(1) == 0)
    def _():
        ng = ng_ref[...]

        def body(r, carry):
            rows = pl.ds(pl.multiple_of(r * ROW_CHUNK, ROW_CHUNK), ROW_CHUNK)
            x = gl_ref[rows, :].astype(F32)
            y = x * lax.rsqrt(jnp.mean(x * x, axis=-1, keepdims=True) + EPS) * ng
            a_scr[rows, 0:lw] = y.astype(BF16)
            return carry

        lax.fori_loop(0, tm // ROW_CHUNK, body, 0)
        a_scr[:, lw:] = yg_ref[...]

    o_ref[...] = jnp.dot(a_scr[...], w_ref[...], preferred_element_type=F32).astype(o_ref.dtype)


def _outproj(g_lru, y_gdn, ng, w_out):
    m, lw = g_lru.shape
    gw = y_gdn.shape[1]
    k, n = w_out.shape
    tm = _tile(m, 1024, SUBLANES)
    tn = _tile(n, 512)
    return pl.pallas_call(
        _outproj_kernel,
        out_shape=jax.ShapeDtypeStruct((m, n), BF16),
        grid=(m // tm, n // tn),
        in_specs=[pl.BlockSpec((tm, lw), lambda i, j: (i, 0)),
                  pl.BlockSpec((tm, gw), lambda i, j: (i, 0)),
                  pl.BlockSpec((1, lw), lambda i, j: (0, 0)),
                  pl.BlockSpec((k, tn), lambda i, j: (0, j))],
        out_specs=pl.BlockSpec((tm, tn), lambda i, j: (i, j)),
        scratch_shapes=[pltpu.VMEM((tm, k), BF16)],
        compiler_params=_cparams(("parallel", "arbitrary")),
        name="outproj",
    )(g_lru, y_gdn, ng, w_out)


def _post_kernel(with_next, *refs):
    if with_next:
        x_ref, y_ref, gp_ref, gate_ref, gn_ref, sh_ref, sc_ref, xo_ref, ho_ref = refs
    else:
        x_ref, y_ref, gp_ref, gate_ref, xo_ref = refs
    y = y_ref[...].astype(F32)
    yn = y * lax.rsqrt(jnp.mean(y * y, axis=-1, keepdims=True) + EPS) * gp_ref[...]
    x1 = x_ref[...] + gate_ref[...] * yn
    xo_ref[...] = x1
    if with_next:
        ho_ref[...] = _norm_mod_rows(x1, gn_ref[...], sc_ref[...], sh_ref[...]).astype(BF16)


def _post(x, y, g_post, gate, nxt=None):
    b, t, d = x.shape
    tr = _tile(t, 256, SUBLANES)
    row = pl.BlockSpec((None, tr, d), lambda bb, i: (bb, i, 0))
    vec = pl.BlockSpec((1, d), lambda bb, i: (0, 0))
    mod = pl.BlockSpec((None, 1, d), lambda bb, i: (bb, 0, 0))
    in_specs = [row, row, vec, mod]
    args = [x, y, g_post, gate]
    out_shape = [jax.ShapeDtypeStruct((b, t, d), F32)]
    out_specs = [row]
    if nxt is not None:
        g_next, sh, sc = nxt
        in_specs += [vec, mod, mod]
        args += [g_next, sh, sc]
        out_shape.append(jax.ShapeDtypeStruct((b, t, d), BF16))
        out_specs.append(row)
    return pl.pallas_call(
        functools.partial(_post_kernel, nxt is not None),
        out_shape=tuple(out_shape),
        grid=(b, t // tr),
        in_specs=in_specs,
        out_specs=tuple(out_specs),
        compiler_params=_cparams(("parallel", "parallel")),
        name="post_next" if nxt is not None else "post",
    )(*args)


def _ffn_kernel(overlap, h_ref, wg_ref, wu_ref, wd_ref, o_ref, acc_ref):
    f = pl.program_id(1)
    last = pl.num_programs(1) - 1

    @pl.when(f == 0)
    def _():
        acc_ref[...] = jnp.zeros(acc_ref.shape, F32)

    h = h_ref[...]
    g = jnp.dot(h, wg_ref[...], preferred_element_type=F32)
    u = jnp.dot(h, wu_ref[...], preferred_element_type=F32)
    act = _silu(g) * u
    if overlap:
        col = lax.broadcasted_iota(jnp.int32, act.shape, 1)
        act = jnp.where(col >= jnp.where(f == last, overlap, 0), act, 0.0)
    acc_ref[...] += jnp.dot(act.astype(BF16), wd_ref[...], preferred_element_type=F32)

    @pl.when(f == last)
    def _():
        o_ref[...] = acc_ref[...].astype(o_ref.dtype)


FFN_HIDDEN_TILE = 512


def _ffn(h, wg, wu, wd):
    m, d = h.shape
    dff = wg.shape[1]
    tm = _tile(m, 512, SUBLANES)
    tf = min(FFN_HIDDEN_TILE, dff)
    n_f = pl.cdiv(dff, tf)
    overlap = n_f * tf - dff
    assert overlap % LANES == 0
    assert tf % LANES == 0 and dff % LANES == 0
    start = lambda f: jnp.minimum(f * (tf // LANES), (dff - tf) // LANES) * LANES
    return pl.pallas_call(
        functools.partial(_ffn_kernel, overlap),
        out_shape=jax.ShapeDtypeStruct((m, d), BF16),
        grid=(m // tm, n_f),
        in_specs=[pl.BlockSpec((tm, d), lambda i, f: (i, 0)),
                  pl.BlockSpec((pl.Element(d), pl.Element(tf)), lambda i, f: (0, start(f))),
                  pl.BlockSpec((pl.Element(d), pl.Element(tf)), lambda i, f: (0, start(f))),
                  pl.BlockSpec((pl.Element(tf), pl.Element(d)), lambda i, f: (start(f), 0))],
        out_specs=pl.BlockSpec((tm, d), lambda i, f: (i, 0)),
        scratch_shapes=[pltpu.VMEM((tm, d), F32)],
        compiler_params=_cparams(("parallel", "arbitrary")),
        name="ffn",
    )(h, wg, wu, wd)


def _pad_cols(w, mult):
    n = w.shape[-1]
    padn = (-n) % mult
    return w if padn == 0 else jnp.pad(w, ((0, 0), (0, padn)))


def kernel(x, c, ctx, c_ctx, w_ada, b_ada, g_pre_mix, g_post_mix, g_pre_ffn, g_post_ffn, w_in,
           lru_conv_w, lru_conv_b, lru_w_a, lru_b_a, lru_w_x, lru_b_x, lru_lambda, lru_norm_g,
           gdn_conv_w, gdn_a_log, gdn_dt_bias, gdn_norm_g, w_out, w_ffn_gate, w_ffn_up, w_ffn_down):
    depth = w_ada.shape[0]
    bsz, seq, d = x.shape
    tctx = ctx.shape[1]
    lw = lru_lambda.shape[-1]
    gh = gdn_a_log.shape[-1]
    hd = gdn_norm_g.shape[-1]
    gw = gh * hd
    n_main = 2 * lw + 4 * gw
    rows = seq // GRID_W
    assert hd == LANES and 4 * gh <= LANES and seq == rows * GRID_W
    assert n_main % gw == 0 and (2 * lw + 3 * gw) % gw == 0

    for l in range(depth):
        assert l == depth - 1, "context stream updates (depth > 1) are not implemented"

        crows = jnp.zeros((SUBLANES, d), F32).at[0:bsz].set(c).at[bsz].set(c_ctx)
        mod = _ada(crows, w_ada[l], b_ada[l][None, :])
        mod_lat = mod[0:bsz].reshape(bsz, N_MOD, 1, d)
        shift_m, scale_m, gate_m, shift_f, scale_f, gate_f = (mod_lat[:, i] for i in range(N_MOD))
        mod_ctx = mod[bsz].reshape(N_MOD, 1, 1, d)

        w_all = w_in[l].astype(BF16)
        w_ba = _pad_cols(w_in[l][:, n_main:], LANES).astype(BF16)
        g_pre = g_pre_mix[l][None, :]
        p_lat, ba_lat = _inproj(_prenorm(x, g_pre, shift_m, scale_m), w_all, n_main, w_ba)
        p_ctx, ba_ctx = _inproj(_prenorm(ctx, g_pre, mod_ctx[0], mod_ctx[1]), w_all, n_main, w_ba)

        g_lru = _lru(p_lat, p_ctx, lru_conv_w[l], lru_conv_b[l][None, :], lru_w_a[l].astype(BF16),
                     lru_b_a[l], lru_w_x[l].astype(BF16), lru_b_x[l], lru_lambda[l], lw)

        qkv_lat = _gdnpre(p_lat, 2 * lw, gdn_conv_w[l], gh, GRID_W)
        qkv_ctx = _gdnpre(p_ctx, 2 * lw, gdn_conv_w[l], gh, 1)
        prm = jnp.zeros((SUBLANES, LANES), F32)
        prm = prm.at[0, 2 * gh:4 * gh].set(gdn_a_log[l].reshape(-1))
        prm = prm.at[1, 2 * gh:4 * gh].set(gdn_dt_bias[l].reshape(-1))
        ba_vis = _to_visit(ba_lat, GRID_W)
        s_f = _gdn_call("ctx", False, 0, gh, qkv_ctx, ba_ctx, prm, tctx)
        s_b = _gdn_call("ctx", True, 1, gh, qkv_ctx, ba_ctx, prm, tctx)
        o_f, wg_b, wu_b = _gdn_call("fwd", False, 0, gh, qkv_lat, ba_vis, prm, rows, s0=s_f,
                                    casts=(w_ffn_gate[l], w_ffn_up[l]))
        o_fb, wd_b, wo_b = _gdn_call("bwd", True, 1, gh, qkv_lat, ba_vis, prm, rows, s0=s_b, o_f=o_f,
                                     casts=(w_ffn_down[l], w_out[l]))
        y_gdn = _gdnpost(o_fb, p_lat, 2 * lw + 3 * gw, gdn_norm_g[l][None, :], gh, GRID_W)
        y_gdn = y_gdn.reshape(bsz * seq, gw)

        mix = _outproj(g_lru.reshape(bsz * seq, lw), y_gdn, lru_norm_g[l][None, :], wo_b)
        x1, h_ffn = _post(x, mix.reshape(bsz, seq, d), g_post_mix[l][None, :], gate_m,
                          nxt=(g_pre_ffn[l][None, :], shift_f, scale_f))
        y_ffn = _ffn(h_ffn.reshape(bsz * seq, d), wg_b, wu_b, wd_b)
        x = _post(x1, y_ffn.reshape(bsz, seq, d), g_post_ffn[l][None, :], gate_f)[0]
    return x
```

```python
import functools

import jax
import jax.numpy as jnp
from jax import lax
from jax.experimental import pallas as pl
from jax.experimental.pallas import tpu as pltpu

F32 = jnp.float32
BF16 = jnp.bfloat16

EPS = 1e-6
F32_TINY = 1.1754944e-38
GRID_W = 64
CHUNK = 64
LRU_C = 8.0
CONV_W = 4
N_MOD = 6
LANES = 128
SUBLANES = 8
VMEM_LIMIT = 56 * 1024 * 1024


def _cparams(sem):
    return pltpu.CompilerParams(dimension_semantics=sem, vmem_limit_bytes=VMEM_LIMIT)


def _mm(a, b):
    return jnp.dot(a.astype(BF16), b.astype(BF16), preferred_element_type=F32)


def _mm_nt(a, b):
    return lax.dot_general(a.astype(BF16), b.astype(BF16), (((1,), (1,)), ((), ())),
                           preferred_element_type=F32)


def _mm_tn(a, b):
    return lax.dot_general(a.astype(BF16), b.astype(BF16), (((0,), (0,)), ((), ())),
                           preferred_element_type=F32)


def _sigmoid(x):
    return 1.0 / (1.0 + jnp.exp(-x))


def _silu(x):
    return x * _sigmoid(x)


def _softplus(x):
    return jnp.maximum(x, 0.0) + jnp.log1p(jnp.exp(-jnp.abs(x)))


def _tile(n, target, mult=LANES):
    if n <= target:
        return n
    t = (target // mult) * mult
    while t >= mult:
        if n % t == 0:
            return t
        t -= mult
    return n


def _ada_kernel(c_ref, w_ref, b_ref, o_ref):
    o_ref[...] = _mm(_silu(c_ref[...]), w_ref[...]) + b_ref[...]


def _ada(crows, w, b):
    r, d = crows.shape
    n = w.shape[1]
    tn = _tile(n, 512)
    return pl.pallas_call(
        _ada_kernel,
        out_shape=jax.ShapeDtypeStruct((r, n), F32),
        grid=(n // tn,),
        in_specs=[pl.BlockSpec((r, d), lambda j: (0, 0)),
                  pl.BlockSpec((d, tn), lambda j: (0, j)),
                  pl.BlockSpec((1, tn), lambda j: (0, j))],
        out_specs=pl.BlockSpec((r, tn), lambda j: (0, j)),
        compiler_params=_cparams(("arbitrary",)),
        name="ada",
    )(crows, w, b)


ROW_CHUNK = 64


def _norm_mod_rows(x, gs, sh):
    return x * lax.rsqrt(jnp.mean(x * x, axis=-1, keepdims=True) + EPS) * gs + sh


NORM_ROWS = 16


def _row_steps(n_rows, body):
    def step(i, carry):
        body(pl.ds(pl.multiple_of(i * NORM_ROWS, NORM_ROWS), NORM_ROWS))
        return carry

    lax.fori_loop(0, n_rows // NORM_ROWS, step, 0, unroll=4)


def _prenorm_kernel(x_ref, g_ref, sh_ref, sc_ref, o_ref):
    gs, sh = g_ref[...] * (1.0 + sc_ref[...]), sh_ref[...]

    def body(rows):
        o_ref[rows, :] = _norm_mod_rows(x_ref[rows, :], gs, sh).astype(BF16)

    _row_steps(x_ref.shape[0], body)


def _prenorm(x, g, sh, sc):
    b, t, d = x.shape
    tr = _tile(t, 256, SUBLANES)
    per_batch = sh.shape[0] > 1
    mod_map = (lambda bb, i: (bb, 0, 0)) if per_batch else (lambda bb, i: (0, 0, 0))
    row = pl.BlockSpec((None, tr, d), lambda bb, i: (bb, i, 0))
    return pl.pallas_call(
        _prenorm_kernel,
        out_shape=jax.ShapeDtypeStruct((b, t, d), BF16),
        grid=(b, t // tr),
        in_specs=[row, pl.BlockSpec((1, d), lambda bb, i: (0, 0)),
                  pl.BlockSpec((None, 1, d), mod_map), pl.BlockSpec((None, 1, d), mod_map)],
        out_specs=row,
        compiler_params=_cparams(("parallel", "parallel")),
        name="prenorm",
    )(x, g, sh, sc)


def _inproj_kernel(h_ref, w_ref, wba_ref, o_ref, oba_ref):
    @pl.when(pl.program_id(2) == 0)
    def _():
        oba_ref[...] = jnp.dot(h_ref[...], wba_ref[...], preferred_element_type=F32)

    o_ref[...] = jnp.dot(h_ref[...], w_ref[...], preferred_element_type=F32)


def _inproj(h, w, n, w_ba):
    b, t, d = h.shape
    nba = w_ba.shape[1]
    tm = _tile(t, 1024, SUBLANES)
    tn = _tile(n, 768 if tm > 512 else 1536)
    return pl.pallas_call(
        _inproj_kernel,
        out_shape=(jax.ShapeDtypeStruct((b, t, n), F32), jax.ShapeDtypeStruct((b, t, nba), F32)),
        grid=(b, t // tm, n // tn),
        in_specs=[pl.BlockSpec((None, tm, d), lambda bb, i, j: (bb, i, 0)),
                  pl.BlockSpec((d, tn), lambda bb, i, j: (0, j)),
                  pl.BlockSpec((d, nba), lambda bb, i, j: (0, 0))],
        out_specs=(pl.BlockSpec((None, tm, tn), lambda bb, i, j: (bb, i, j)),
                   pl.BlockSpec((None, tm, nba), lambda bb, i, j: (bb, i, 0))),
        compiler_params=_cparams(("parallel", "parallel", "arbitrary")),
        name="inproj",
    )(h, w, w_ba)


def _fill_padded(xp_ref, x_ref, n, pad):
    zeros = jnp.zeros((pad, xp_ref.shape[1]), F32)
    xp_ref[0:pad, :] = zeros
    xp_ref[pad + n:pad + n + pad, :] = zeros
    step = min(n, 512)

    def body(i, carry):
        rows = pl.ds(pl.multiple_of(i * step, step), step)
        xp_ref[pl.ds(pl.multiple_of(pad + i * step, SUBLANES), step), :] = x_ref[rows, :]
        return carry

    lax.fori_loop(0, n // step, body, 0)


def _row_pitch(stride):
    return stride if stride == 1 else stride + SUBLANES


def _conv_rows(xp_ref, cw_ref, t0, rows, pad, stride):
    acc = None
    for j in range(CONV_W):
        off = pad + (j - 2) * stride
        term = cw_ref[j:j + 1, :] * xp_ref[pl.ds(t0 + off, rows), :]
        acc = term if acc is None else acc + term
    return acc


LRU_ROWS = 2048
LRU_PAD = 8


def _lru_coeffs(xc, d, wa_ref, ba_ref, wx_ref, bx_ref, lam_ref):
    xb = xc.astype(BF16)
    r = _sigmoid(jnp.dot(xb, wa_ref[d], preferred_element_type=F32) + ba_ref[d:d + 1, :])
    i = _sigmoid(jnp.dot(xb, wx_ref[d], preferred_element_type=F32) + bx_ref[d:d + 1, :])
    log_a = r * ((-LRU_C) * _softplus(-lam_ref[d:d + 1, :]))
    a = jnp.exp(log_a)
    th = jnp.tanh(log_a)
    y = -2.0 * th / (1.0 - th)
    return a, (y * lax.rsqrt(jnp.maximum(y, F32_TINY))) * (i * xc)


def _lru_scan_chunk(a, b, h, rev):
    rows, c = a.shape
    nt = rows // SUBLANES
    a3 = a.reshape(nt, SUBLANES, c)
    b3 = b.reshape(nt, SUBLANES, c)
    ridx = lax.broadcasted_iota(jnp.int32, (nt, SUBLANES, c), 1)
    for s in (1, 2, 4):
        shift = (SUBLANES - s) if rev else s
        a_s = pltpu.roll(a3, shift, 1)
        b_s = pltpu.roll(b3, shift, 1)
        m = (ridx < SUBLANES - s) if rev else (ridx >= s)
        b3 = jnp.where(m, a3 * b_s + b3, b3)
        a3 = jnp.where(m, a3 * a_s, a3)
    out = [None] * nt
    order = range(nt - 1, -1, -1) if rev else range(nt)
    for t in order:
        ht = b3[t] + a3[t] * h
        out[t] = ht
        h = ht[0:1, :] if rev else ht[SUBLANES - 1:SUBLANES, :]
    return jnp.concatenate(out, axis=0), h


def _lru_kernel(ul_ref, gl_ref, uc_ref, cw_ref, cb_ref, wa_ref, ba_ref, wx_ref, bx_ref, lam_ref,
                o_ref, xpl_ref, xpc_ref, xc_ref, hf_ref):
    t_lat = ul_ref.shape[0]
    t_ctx = uc_ref.shape[0]
    c = ul_ref.shape[1]
    rl = min(LRU_ROWS, t_lat)
    rc = min(LRU_ROWS, t_ctx)
    _fill_padded(xpl_ref, ul_ref, t_lat, LRU_PAD)
    _fill_padded(xpc_ref, uc_ref, t_ctx, LRU_PAD)
    cb = cb_ref[...]

    for d in range(2):
        rev = d == 1
        prm = (wa_ref, ba_ref, wx_ref, bx_ref, lam_ref)

        def ctx_body(i, h, rev=rev, d=d, prm=prm):
            blk = (t_ctx // rc - 1 - i) if rev else i
            t0 = pl.multiple_of(blk * rc, rc)
            xc = _conv_rows(xpc_ref, cw_ref, t0, rc, LRU_PAD, 1) + cb
            a, b = _lru_coeffs(xc, d, *prm)
            _, h = _lru_scan_chunk(a, b, h, rev)
            return h

        h = lax.fori_loop(0, t_ctx // rc, ctx_body, jnp.zeros((1, c), F32))

        def lat_body(i, h, rev=rev, d=d, prm=prm):
            blk = (t_lat // rl - 1 - i) if rev else i
            t0 = pl.multiple_of(blk * rl, rl)
            rows = pl.ds(t0, rl)
            if rev:
                xc = xc_ref[rows, :]
            else:
                xc = _conv_rows(xpl_ref, cw_ref, t0, rl, LRU_PAD, 1) + cb
                xc_ref[rows, :] = xc
            a, b = _lru_coeffs(xc, d, *prm)
            hs, h = _lru_scan_chunk(a, b, h, rev)
            if rev:
                y = (hf_ref[rows, :] + hs) * jax.nn.gelu(gl_ref[rows, :], approximate=True)
                o_ref[rows, :] = y.astype(BF16)
            else:
                hf_ref[rows, :] = hs
            return h

        lax.fori_loop(0, t_lat // rl, lat_body, h)


def _lru(p_lat, p_ctx, conv_w, conv_b, w_a, b_a, w_x, b_x, lam, lw):
    b, t, _ = p_lat.shape
    tc = p_ctx.shape[1]
    nb, bd = w_a.shape[1], w_a.shape[2]
    assert bd == LANES and nb * bd == lw
    vec = lambda bb, n: (0, n)
    return pl.pallas_call(
        _lru_kernel,
        out_shape=jax.ShapeDtypeStruct((b, t, lw), BF16),
        grid=(b, nb),
        in_specs=[pl.BlockSpec((None, t, bd), lambda bb, n: (bb, 0, n)),
                  pl.BlockSpec((None, t, bd), lambda bb, n: (bb, 0, nb + n)),
                  pl.BlockSpec((None, tc, bd), lambda bb, n: (bb, 0, n)),
                  pl.BlockSpec((CONV_W, bd), vec),
                  pl.BlockSpec((1, bd), vec),
                  pl.BlockSpec((2, None, bd, bd), lambda bb, n: (0, n, 0, 0)),
                  pl.BlockSpec((2, bd), vec),
                  pl.BlockSpec((2, None, bd, bd), lambda bb, n: (0, n, 0, 0)),
                  pl.BlockSpec((2, bd), vec),
                  pl.BlockSpec((2, bd), vec)],
        out_specs=pl.BlockSpec((None, t, bd), lambda bb, n: (bb, 0, n)),
        scratch_shapes=[pltpu.VMEM((t + 2 * LRU_PAD, bd), F32),
                        pltpu.VMEM((tc + 2 * LRU_PAD, bd), F32),
                        pltpu.VMEM((t, bd), F32),
                        pltpu.VMEM((t, bd), F32)],
        compiler_params=_cparams(("parallel", "parallel")),
        name="lru",
    )(p_lat, p_lat, p_ctx, conv_w, conv_b, w_a, b_a, w_x, b_x, lam)


def _gdnpre_kernel(stride, n_heads, x_ref, cw_ref, o_ref, xp_ref, y_ref):
    t = x_ref.shape[0]
    pad = max(2 * stride, SUBLANES)
    _fill_padded(xp_ref, x_ref, t, pad)
    step = min(t, 512)
    pitch = _row_pitch(stride)
    per = step // stride if stride > 1 else 1

    def conv_body(i, carry):
        t0 = pl.multiple_of(i * step, step)
        y = _conv_rows(xp_ref, cw_ref, t0, step, pad, stride)
        if stride == 1:
            y_ref[pl.ds(t0, step), :] = y
        else:
            for j in range(per):
                dst = pl.multiple_of((i * per + j) * pitch, SUBLANES)
                y_ref[pl.ds(dst, stride), :] = y[j * stride:(j + 1) * stride, :]
        return carry

    lax.fori_loop(0, t // step, conv_body, 0)

    if stride > 1:
        w = stride
        n_rows = t // w
        last = t - w
        ylast = (n_rows - 1) * pitch
        col = lax.broadcasted_iota(jnp.int32, (w, 1), 0)
        not_first = col >= 1
        not_last = col <= w - 2
        prev_bot = xp_ref[pl.ds(pad + last - 1, w), :]
        prev_bot2 = xp_ref[pl.ds(pad + last - w - 1, w), :]
        next_top = xp_ref[pl.ds(pad + 1, w), :]
        y_ref[0:w, :] = y_ref[0:w, :] + jnp.where(
            not_first, cw_ref[1:2, :] * prev_bot + cw_ref[0:1, :] * prev_bot2, 0.0)
        y_ref[pitch:pitch + w, :] = y_ref[pitch:pitch + w, :] + jnp.where(
            not_first, cw_ref[0:1, :] * prev_bot, 0.0)
        y_ref[ylast:ylast + w, :] = y_ref[ylast:ylast + w, :] + jnp.where(
            not_last, cw_ref[3:4, :] * next_top, 0.0)

    blk = pl.program_id(1)
    hd = x_ref.shape[1]

    def finish(normed, scale):
        def act(y):
            y = _silu(y)
            if normed:
                y = y * (lax.rsqrt(jnp.sum(y * y, axis=-1, keepdims=True) + EPS) * scale)
            return y.astype(BF16)

        if stride == 1:
            def body(i, carry):
                rows = pl.ds(pl.multiple_of(i * step, step), step)
                o_ref[rows, :] = act(y_ref[rows, :])
                return carry

            lax.fori_loop(0, t // step, body, 0)
        else:
            n_rows = t // stride

            def body(w, carry):
                dst = pl.ds(pl.multiple_of(w * n_rows, n_rows), n_rows)
                o_ref[dst, :] = act(y_ref[pl.ds(w, n_rows, stride=pitch), :])
                return carry

            lax.fori_loop(0, stride, body, 0, unroll=8)

    @pl.when(blk < n_heads)
    def _():
        finish(True, hd ** -0.5)

    @pl.when(jnp.logical_and(blk >= n_heads, blk < 2 * n_heads))
    def _():
        finish(True, 1.0)

    @pl.when(blk >= 2 * n_heads)
    def _():
        finish(False, 1.0)


def _gdnpre(p, col0, conv_w, n_heads, stride):
    b, t, _ = p.shape
    hd = LANES
    nblk = 3 * n_heads
    assert col0 % hd == 0 and t % stride == 0
    pad = max(2 * stride, SUBLANES)
    y_rows = (t // stride) * _row_pitch(stride)
    return pl.pallas_call(
        functools.partial(_gdnpre_kernel, stride, n_heads),
        out_shape=jax.ShapeDtypeStruct((b, t, nblk * hd), BF16),
        grid=(b, nblk),
        in_specs=[pl.BlockSpec((None, t, hd), lambda bb, n: (bb, 0, col0 // hd + n)),
                  pl.BlockSpec((CONV_W, hd), lambda bb, n: (0, n))],
        out_specs=pl.BlockSpec((None, t, hd), lambda bb, n: (bb, 0, n)),
        scratch_shapes=[pltpu.VMEM((t + 2 * pad, hd), F32), pltpu.VMEM((y_rows, hd), F32)],
        compiler_params=_cparams(("parallel", "parallel")),
        name="gdnpre",
    )(p, conv_w)


GDN_GROUP_ITEMS = 32


def _split3(x):
    hi = x.astype(BF16)
    r1 = x - hi.astype(F32)
    mid = r1.astype(BF16)
    lo = (r1 - mid.astype(F32)).astype(BF16)
    return hi, mid, lo


def _gdn_kernel(mode, rev, d, gh, n_cast, *refs):
    n_in = {"ctx": 5, "fwd": 6, "bwd": 7}[mode]
    cast_in = refs[n_in:n_in + n_cast]
    cast_out = refs[n_in + n_cast + 1:n_in + 2 * n_cast + 1]
    refs = refs[:n_in] + refs[n_in + n_cast:n_in + n_cast + 1] + refs[n_in + 2 * n_cast + 1:]
    if mode == "ctx":
        q_ref, k_ref, v_ref, ba_ref, prm_ref, sfin_ref, s_scr = refs
    elif mode == "fwd":
        q_ref, k_ref, v_ref, ba_ref, prm_ref, s0_ref, o_ref, s_scr = refs
    else:
        q_ref, k_ref, v_ref, ba_ref, prm_ref, s0_ref, of_ref, o_ref, s_scr = refs
    for src, dst in zip(cast_in, cast_out):
        dst[...] = src[...].astype(BF16)
    r = q_ref.shape[0]
    hd = LANES
    nch = r // CHUNK
    step = pl.program_id(1)
    nsteps = pl.num_programs(1)

    @pl.when(step == 0)
    def _():
        if mode == "ctx":
            s_scr[...] = jnp.zeros(s_scr.shape, F32)
        else:
            s_scr[...] = s0_ref[...]

    ba = ba_ref[...]
    sig = _sigmoid(ba)
    gfull = -jnp.exp(prm_ref[0:1, :]) * _softplus(ba + prm_ref[1:2, :])
    ii = lax.broadcasted_iota(jnp.int32, (r, r), 0)
    jj = lax.broadcasted_iota(jnp.int32, (r, r), 1)
    same = (ii // CHUNK) == (jj // CHUNK)
    tri = jnp.where(jnp.logical_and(same, (jj >= ii) if rev else (jj <= ii)), 1.0, 0.0).astype(BF16)
    g_hi, g_mid, g_lo = _split3(gfull)
    gc = (jnp.dot(tri, g_hi, preferred_element_type=F32)
          + jnp.dot(tri, g_mid, preferred_element_type=F32)
          + jnp.dot(tri, g_lo, preferred_element_type=F32))
    sig_t = sig.T
    gc_t = gc.T
    egc = jnp.exp(gc)

    ci = lax.broadcasted_iota(jnp.int32, (CHUNK, CHUNK), 0)
    cj = lax.broadcasted_iota(jnp.int32, (CHUNK, CHUNK), 1)
    low = (ci <= cj) if rev else (ci >= cj)
    strict = (ci < cj) if rev else (ci > cj)
    eye = jnp.where(ci == cj, 1.0, 0.0).astype(F32)
    last = 0 if rev else CHUNK - 1
    chunk_order = list(range(nch - 1, -1, -1) if rev else range(nch))
    heads_per_group = max(1, min(gh, GDN_GROUP_ITEMS // nch))

    for g0 in range(0, gh, heads_per_group):
        heads = list(range(g0, min(gh, g0 + heads_per_group)))
        items = [(h, c) for h in heads for c in range(nch)]
        st = {}
        for (h, c) in items:
            rows = slice(c * CHUNK, (c + 1) * CHUNK)
            lanes = slice(h * hd, (h + 1) * hd)
            cb = d * gh + h
            cg = 2 * gh + d * gh + h
            bcol = sig[rows, cb:cb + 1]
            gcol = gc[rows, cg:cg + 1]
            grow = gc_t[cg:cg + 1, rows]
            brow = sig_t[cb:cb + 1, rows]
            kc = k_ref[rows, lanes]
            decay = jnp.where(low, jnp.exp(jnp.where(low, gcol - grow, 0.0)), 0.0)
            x = -jnp.where(strict, _mm_nt(kc, kc) * bcol * decay, 0.0)
            it = dict(rows=rows, lanes=lanes, kc=kc, gcol=gcol, brow=brow, grow=grow,
                      egcol=egc[rows, cg:cg + 1], xb=x.astype(BF16), t=eye + x)
            if mode != "ctx":
                it["intra"] = (_mm_nt(q_ref[rows, lanes], kc) * decay).astype(BF16)
            st[(h, c)] = it
        for _ in range(5):
            for it in st.values():
                it["xb"] = jnp.dot(it["xb"], it["xb"], preferred_element_type=F32).astype(BF16)
            for it in st.values():
                it["t"] = it["t"] + jnp.dot(it["t"].astype(BF16), it["xb"],
                                            preferred_element_type=F32)
        for it in st.values():
            tb = it["t"] * it["brow"]
            it["u"] = _mm(tb, v_ref[it["rows"], it["lanes"]])
            it["w"] = _mm(tb * jnp.exp(it["grow"]), it["kc"]).astype(BF16)
            del it["t"], it["xb"]
        state = {h: s_scr[h] for h in heads}
        for c in chunk_order:
            cur = [st[(h, c)] for h in heads]
            sb = [state[h].astype(BF16) for h in heads]
            ws = [jnp.dot(it["w"], s, preferred_element_type=F32) for it, s in zip(cur, sb)]
            if mode != "ctx":
                qs = [jnp.dot(q_ref[it["rows"], it["lanes"]], s, preferred_element_type=F32)
                      for it, s in zip(cur, sb)]
            vnew = [it["u"] - x for it, x in zip(cur, ws)]
            glast = [it["gcol"][last:last + 1, :] for it in cur]
            if mode != "ctx":
                iv = [_mm(it["intra"], v) for it, v in zip(cur, vnew)]
            kv = [_mm_tn(it["kc"], v * jnp.exp(gl - it["gcol"]))
                  for it, v, gl in zip(cur, vnew, glast)]
            for i, h in enumerate(heads):
                it = cur[i]
                if mode != "ctx":
                    o = it["egcol"] * qs[i] + iv[i]
                    if mode == "bwd":
                        o = o + of_ref[it["rows"], it["lanes"]]
                    o_ref[it["rows"], it["lanes"]] = o
                state[h] = state[h] * jnp.exp(glast[i]) + kv[i]
        for h in heads:
            s_scr[h] = state[h]

    if mode == "ctx":
        @pl.when(step == nsteps - 1)
        def _():
            sfin_ref[...] = s_scr[...]


def _gdn_call(mode, rev, d, gh, qkv, ba, prm, r, s0=None, o_f=None, casts=()):
    b, t, _ = qkv.shape
    hd = LANES
    gw = gh * hd
    n_blk = t // r
    assert t % r == 0 and r % CHUNK == 0 and r % LANES == 0
    blk = (lambda w: n_blk - 1 - w) if rev else (lambda w: w)
    in_specs = [pl.BlockSpec((None, r, gw), lambda bb, w: (bb, blk(w), 0)),
                pl.BlockSpec((None, r, gw), lambda bb, w: (bb, blk(w), 1)),
                pl.BlockSpec((None, r, gw), lambda bb, w: (bb, blk(w), 2)),
                pl.BlockSpec((None, r, LANES), lambda bb, w: (bb, blk(w), 0)),
                pl.BlockSpec((SUBLANES, LANES), lambda bb, w: (0, 0))]
    args = [qkv, qkv, qkv, ba, prm]
    state_spec = pl.BlockSpec((None, gh, hd, hd), lambda bb, w: (bb, 0, 0, 0))
    o_spec = pl.BlockSpec((None, r, gw), lambda bb, w: (bb, blk(w), 0))
    if mode != "ctx":
        in_specs.append(state_spec)
        args.append(s0)
    if mode == "bwd":
        in_specs.append(o_spec)
        args.append(o_f)
    if mode == "ctx":
        out_shape = [jax.ShapeDtypeStruct((b, gh, hd, hd), F32)]
        out_specs = [state_spec]
    else:
        out_shape = [jax.ShapeDtypeStruct((b, t, gw), F32)]
        out_specs = [o_spec]
    n_steps = b * n_blk
    for wmat in casts:
        nr, nc = wmat.shape
        rp = _cast_rows(nr, n_steps)
        last = nr // rp - 1
        spec = pl.BlockSpec((rp, nc), lambda bb, w, last=last: (jnp.minimum(bb * n_blk + w, last), 0))
        in_specs.append(spec)
        args.append(wmat)
        out_shape.append(jax.ShapeDtypeStruct((nr, nc), BF16))
        out_specs.append(spec)
    outs = pl.pallas_call(
        functools.partial(_gdn_kernel, mode, rev, d, gh, len(casts)),
        out_shape=tuple(out_shape),
        grid=(b, n_blk),
        in_specs=in_specs,
        out_specs=tuple(out_specs),
        scratch_shapes=[pltpu.VMEM((gh, hd, hd), F32)],
        compiler_params=_cparams(("arbitrary", "arbitrary")),
        name="gdn_%s_%d" % (mode, d),
    )(*args)
    return outs[0] if not casts else outs


def _cast_rows(n_rows, n_steps):
    for rp in range(2 * SUBLANES, n_rows, 2 * SUBLANES):
        if n_rows % rp == 0 and n_rows // rp <= n_steps:
            return rp
    return n_rows


def _to_visit_kernel(width, x_ref, o_ref):
    n_rows = x_ref.shape[0] // width

    def body(w, carry):
        dst = pl.ds(pl.multiple_of(w * n_rows, n_rows), n_rows)
        o_ref[dst, :] = x_ref[pl.ds(w, n_rows, stride=width), :]
        return carry

    lax.fori_loop(0, width, body, 0)


def _to_visit(x, width):
    b, t, c = x.shape
    assert c == LANES
    spec = pl.BlockSpec((None, t, c), lambda bb: (bb, 0, 0))
    return pl.pallas_call(
        functools.partial(_to_visit_kernel, width),
        out_shape=jax.ShapeDtypeStruct((b, t, c), F32),
        grid=(b,),
        in_specs=[spec],
        out_specs=spec,
        compiler_params=_cparams(("parallel",)),
        name="to_visit",
    )(x)


def _gdnpost_kernel(width, o_ref, z_ref, ng_ref, y_ref, scr_ref):
    t = o_ref.shape[0]
    n_rows = t // width
    pitch = _row_pitch(width)

    def scatter(w, carry):
        src = pl.ds(pl.multiple_of(w * n_rows, n_rows), n_rows)
        scr_ref[pl.ds(w, n_rows, stride=pitch), :] = o_ref[src, :]
        return carry

    lax.fori_loop(0, width, scatter, 0)
    per = SUBLANES
    ng = ng_ref[...]

    def body(i, carry):
        parts = [scr_ref[pl.ds(pl.multiple_of((i * per + j) * pitch, SUBLANES), width), :]
                 for j in range(per)]
        o = jnp.concatenate(parts, axis=0)
        rows = pl.ds(pl.multiple_of(i * per * width, per * width), per * width)
        y = o * lax.rsqrt(jnp.mean(o * o, axis=-1, keepdims=True) + EPS) * ng
        y_ref[rows, :] = (y * _silu(z_ref[rows, :])).astype(BF16)
        return carry

    lax.fori_loop(0, n_rows // per, body, 0)


def _gdnpost(o, p, zcol0, ng, gh, width):
    b, t, gw = o.shape
    hd = LANES
    n_rows = t // width
    assert zcol0 % hd == 0 and n_rows % SUBLANES == 0
    return pl.pallas_call(
        functools.partial(_gdnpost_kernel, width),
        out_shape=jax.ShapeDtypeStruct((b, t, gw), BF16),
        grid=(b, gh),
        in_specs=[pl.BlockSpec((None, t, hd), lambda bb, h: (bb, 0, h)),
                  pl.BlockSpec((None, t, hd), lambda bb, h: (bb, 0, zcol0 // hd + h)),
                  pl.BlockSpec((1, hd), lambda bb, h: (0, 0))],
        out_specs=pl.BlockSpec((None, t, hd), lambda bb, h: (bb, 0, h)),
        scratch_shapes=[pltpu.VMEM((n_rows * _row_pitch(width), hd), F32)],
        compiler_params=_cparams(("parallel", "parallel")),
        name="gdnpost",
    )(o, p, ng)


def _outproj_kernel(gl_ref, yg_ref, ng_ref, w_ref, o_ref, a_scr):
    tm, lw = gl_ref.shape

    @pl.when(pl.program_id(1) == 0)
    def _():
        ng = ng_ref[...]

        def body(r, carry):
            rows = pl.ds(pl.multiple_of(r * ROW_CHUNK, ROW_CHUNK), ROW_CHUNK)
            x = gl_ref[rows, :].astype(F32)
            y = x * lax.rsqrt(jnp.mean(x * x, axis=-1, keepdims=True) + EPS) * ng
            a_scr[rows, 0:lw] = y.astype(BF16)
            return carry

        lax.fori_loop(0, tm // ROW_CHUNK, body, 0)
        a_scr[:, lw:] = yg_ref[...]

    o_ref[...] = jnp.dot(a_scr[...], w_ref[...], preferred_element_type=F32).astype(o_ref.dtype)


def _outproj(g_lru, y_gdn, ng, w_out):
    m, lw = g_lru.shape
    gw = y_gdn.shape[1]
    k, n = w_out.shape
    tm = _tile(m, 1024, SUBLANES)
    tn = _tile(n, 512)
    return pl.pallas_call(
        _outproj_kernel,
        out_shape=jax.ShapeDtypeStruct((m, n), BF16),
        grid=(m // tm, n // tn),
        in_specs=[pl.BlockSpec((tm, lw), lambda i, j: (i, 0)),
                  pl.BlockSpec((tm, gw), lambda i, j: (i, 0)),
                  pl.BlockSpec((1, lw), lambda i, j: (0, 0)),
                  pl.BlockSpec((k, tn), lambda i, j: (0, j))],
        out_specs=pl.BlockSpec((tm, tn), lambda i, j: (i, j)),
        scratch_shapes=[pltpu.VMEM((tm, k), BF16)],
        compiler_params=_cparams(("parallel", "arbitrary")),
        name="outproj",
    )(g_lru, y_gdn, ng, w_out)


def _post_kernel(with_next, *refs):
    if with_next:
        x_ref, y_ref, gp_ref, gate_ref, gn_ref, sh_ref, sc_ref, xo_ref, ho_ref = refs
    else:
        x_ref, y_ref, gp_ref, gate_ref, xo_ref = refs
    gpg = gp_ref[...] * gate_ref[...]
    if with_next:
        gs, sh = gn_ref[...] * (1.0 + sc_ref[...]), sh_ref[...]

    def body(rows):
        y = y_ref[rows, :].astype(F32)
        x1 = x_ref[rows, :] + y * lax.rsqrt(jnp.mean(y * y, axis=-1, keepdims=True) + EPS) * gpg
        xo_ref[rows, :] = x1
        if with_next:
            ho_ref[rows, :] = _norm_mod_rows(x1, gs, sh).astype(BF16)

    _row_steps(x_ref.shape[0], body)


def _post(x, y, g_post, gate, nxt=None):
    b, t, d = x.shape
    tr = _tile(t, 256, SUBLANES)
    row = pl.BlockSpec((None, tr, d), lambda bb, i: (bb, i, 0))
    vec = pl.BlockSpec((1, d), lambda bb, i: (0, 0))
    mod = pl.BlockSpec((None, 1, d), lambda bb, i: (bb, 0, 0))
    in_specs = [row, row, vec, mod]
    args = [x, y, g_post, gate]
    out_shape = [jax.ShapeDtypeStruct((b, t, d), F32)]
    out_specs = [row]
    if nxt is not None:
        g_next, sh, sc = nxt
        in_specs += [vec, mod, mod]
        args += [g_next, sh, sc]
        out_shape.append(jax.ShapeDtypeStruct((b, t, d), BF16))
        out_specs.append(row)
    return pl.pallas_call(
        functools.partial(_post_kernel, nxt is not None),
        out_shape=tuple(out_shape),
        grid=(b, t // tr),
        in_specs=in_specs,
        out_specs=tuple(out_specs),
        compiler_params=_cparams(("parallel", "parallel")),
        name="post_next" if nxt is not None else "post",
    )(*args)


def _ffn_kernel(overlap, h_ref, wg_ref, wu_ref, wd_ref, o_ref, acc_ref):
    f = pl.program_id(1)
    last = pl.num_programs(1) - 1

    @pl.when(f == 0)
    def _():
        acc_ref[...] = jnp.zeros(acc_ref.shape, F32)

    h = h_ref[...]
    g = jnp.dot(h, wg_ref[...], preferred_element_type=F32)
    u = jnp.dot(h, wu_ref[...], preferred_element_type=F32)
    act = _silu(g) * u
    if overlap:
        col = lax.broadcasted_iota(jnp.int32, act.shape, 1)
        act = jnp.where(col >= jnp.where(f == last, overlap, 0), act, 0.0)
    acc_ref[...] += jnp.dot(act.astype(BF16), wd_ref[...], preferred_element_type=F32)

    @pl.when(f == last)
    def _():
        o_ref[...] = acc_ref[...].astype(o_ref.dtype)


FFN_HIDDEN_TILE = 512


def _ffn(h, wg, wu, wd):
    m, d = h.shape
    dff = wg.shape[1]
    tm = _tile(m, 512, SUBLANES)
    tf = min(FFN_HIDDEN_TILE, dff)
    n_f = pl.cdiv(dff, tf)
    overlap = n_f * tf - dff
    assert overlap % LANES == 0
    assert tf % LANES == 0 and dff % LANES == 0
    start = lambda f: jnp.minimum(f * (tf // LANES), (dff - tf) // LANES) * LANES
    return pl.pallas_call(
        functools.partial(_ffn_kernel, overlap),
        out_shape=jax.ShapeDtypeStruct((m, d), BF16),
        grid=(m // tm, n_f),
        in_specs=[pl.BlockSpec((tm, d), lambda i, f: (i, 0)),
                  pl.BlockSpec((pl.Element(d), pl.Element(tf)), lambda i, f: (0, start(f))),
                  pl.BlockSpec((pl.Element(d), pl.Element(tf)), lambda i, f: (0, start(f))),
                  pl.BlockSpec((pl.Element(tf), pl.Element(d)), lambda i, f: (start(f), 0))],
        out_specs=pl.BlockSpec((tm, d), lambda i, f: (i, 0)),
        scratch_shapes=[pltpu.VMEM((tm, d), F32)],
        compiler_params=_cparams(("parallel", "arbitrary")),
        name="ffn",
    )(h, wg, wu, wd)


def _pad_cols(w, mult):
    n = w.shape[-1]
    padn = (-n) % mult
    return w if padn == 0 else jnp.pad(w, ((0, 0), (0, padn)))


def kernel(x, c, ctx, c_ctx, w_ada, b_ada, g_pre_mix, g_post_mix, g_pre_ffn, g_post_ffn, w_in,
           lru_conv_w, lru_conv_b, lru_w_a, lru_b_a, lru_w_x, lru_b_x, lru_lambda, lru_norm_g,
           gdn_conv_w, gdn_a_log, gdn_dt_bias, gdn_norm_g, w_out, w_ffn_gate, w_ffn_up, w_ffn_down):
    depth = w_ada.shape[0]
    bsz, seq, d = x.shape
    tctx = ctx.shape[1]
    lw = lru_lambda.shape[-1]
    gh = gdn_a_log.shape[-1]
    hd = gdn_norm_g.shape[-1]
    gw = gh * hd
    n_main = 2 * lw + 4 * gw
    rows = seq // GRID_W
    assert hd == LANES and 4 * gh <= LANES and seq == rows * GRID_W
    assert n_main % gw == 0 and (2 * lw + 3 * gw) % gw == 0

    for l in range(depth):
        assert l == depth - 1, "context stream updates (depth > 1) are not implemented"

        crows = jnp.zeros((SUBLANES, d), F32).at[0:bsz].set(c).at[bsz].set(c_ctx)
        mod = _ada(crows, w_ada[l], b_ada[l][None, :])
        mod_lat = mod[0:bsz].reshape(bsz, N_MOD, 1, d)
        shift_m, scale_m, gate_m, shift_f, scale_f, gate_f = (mod_lat[:, i] for i in range(N_MOD))
        mod_ctx = mod[bsz].reshape(N_MOD, 1, 1, d)

        w_all = w_in[l].astype(BF16)
        w_ba = _pad_cols(w_in[l][:, n_main:], LANES).astype(BF16)
        g_pre = g_pre_mix[l][None, :]
        p_lat, ba_lat = _inproj(_prenorm(x, g_pre, shift_m, scale_m), w_all, n_main, w_ba)
        h_ctx = _prenorm(ctx, g_pre, mod_ctx[0], mod_ctx[1]).reshape(1, bsz * tctx, d)
        p_ctx, ba_ctx = _inproj(h_ctx, w_all, n_main, w_ba)
        p_ctx = p_ctx.reshape(bsz, tctx, n_main)
        ba_ctx = ba_ctx.reshape(bsz, tctx, LANES)

        g_lru = _lru(p_lat, p_ctx, lru_conv_w[l], lru_conv_b[l][None, :], lru_w_a[l].astype(BF16),
                     lru_b_a[l], lru_w_x[l].astype(BF16), lru_b_x[l], lru_lambda[l], lw)

        qkv_lat = _gdnpre(p_lat, 2 * lw, gdn_conv_w[l], gh, GRID_W)
        qkv_ctx = _gdnpre(p_ctx, 2 * lw, gdn_conv_w[l], gh, 1)
        prm = jnp.zeros((SUBLANES, LANES), F32)
        prm = prm.at[0, 2 * gh:4 * gh].set(gdn_a_log[l].reshape(-1))
        prm = prm.at[1, 2 * gh:4 * gh].set(gdn_dt_bias[l].reshape(-1))
        ba_vis = _to_visit(ba_lat, GRID_W)
        s_f = _gdn_call("ctx", False, 0, gh, qkv_ctx, ba_ctx, prm, tctx)
        s_b = _gdn_call("ctx", True, 1, gh, qkv_ctx, ba_ctx, prm, tctx)
        o_f, wg_b, wu_b = _gdn_call("fwd", False, 0, gh, qkv_lat, ba_vis, prm, rows, s0=s_f,
                                    casts=(w_ffn_gate[l], w_ffn_up[l]))
        o_fb, wd_b, wo_b = _gdn_call("bwd", True, 1, gh, qkv_lat, ba_vis, prm, rows, s0=s_b, o_f=o_f,
                                     casts=(w_ffn_down[l], w_out[l]))
        y_gdn = _gdnpost(o_fb, p_lat, 2 * lw + 3 * gw, gdn_norm_g[l][None, :], gh, GRID_W)
        y_gdn = y_gdn.reshape(bsz * seq, gw)

        mix = _outproj(g_lru.reshape(bsz * seq, lw), y_gdn, lru_norm_g[l][None, :], wo_b)
        x1, h_ffn = _post(x, mix.reshape(bsz, seq, d), g_post_mix[l][None, :], gate_m,
                          nxt=(g_pre_ffn[l][None, :], shift_f, scale_f))
        y_ffn = _ffn(h_ffn.reshape(bsz * seq, d), wg_b, wu_b, wd_b)
        x = _post(x1, y_ffn.reshape(bsz, seq, d), g_post_ffn[l][None, :], gate_f)[0]
    return x
```

```python
import functools

import jax
import jax.numpy as jnp
from jax import lax
from jax.experimental import pallas as pl
from jax.experimental.pallas import tpu as pltpu

F32 = jnp.float32
BF16 = jnp.bfloat16

EPS = 1e-6
F32_TINY = 1.1754944e-38
GRID_W = 64
CHUNK = 64
LRU_C = 8.0
CONV_W = 4
N_MOD = 6
LANES = 128
SUBLANES = 8
VMEM_LIMIT = 56 * 1024 * 1024


def _cparams(sem):
    return pltpu.CompilerParams(dimension_semantics=sem, vmem_limit_bytes=VMEM_LIMIT)


def _mm(a, b):
    return jnp.dot(a.astype(BF16), b.astype(BF16), preferred_element_type=F32)


def _mm_nt(a, b):
    return lax.dot_general(a.astype(BF16), b.astype(BF16), (((1,), (1,)), ((), ())),
                           preferred_element_type=F32)


def _mm_tn(a, b):
    return lax.dot_general(a.astype(BF16), b.astype(BF16), (((0,), (0,)), ((), ())),
                           preferred_element_type=F32)


def _sigmoid(x):
    return 1.0 / (1.0 + jnp.exp(-x))


def _silu(x):
    return x * _sigmoid(x)


def _softplus(x):
    return jnp.maximum(x, 0.0) + jnp.log1p(jnp.exp(-jnp.abs(x)))


def _tile(n, target, mult=LANES):
    if n <= target:
        return n
    t = (target // mult) * mult
    while t >= mult:
        if n % t == 0:
            return t
        t -= mult
    return n


def _ada_kernel(c_ref, w_ref, b_ref, o_ref):
    o_ref[...] = _mm(_silu(c_ref[...]), w_ref[...]) + b_ref[...]


def _ada(crows, w, b):
    r, d = crows.shape
    n = w.shape[1]
    tn = _tile(n, 512)
    return pl.pallas_call(
        _ada_kernel,
        out_shape=jax.ShapeDtypeStruct((r, n), F32),
        grid=(n // tn,),
        in_specs=[pl.BlockSpec((r, d), lambda j: (0, 0)),
                  pl.BlockSpec((d, tn), lambda j: (0, j)),
                  pl.BlockSpec((1, tn), lambda j: (0, j))],
        out_specs=pl.BlockSpec((r, tn), lambda j: (0, j)),
        compiler_params=_cparams(("arbitrary",)),
        name="ada",
    )(crows, w, b)


ROW_CHUNK = 64


def _norm_mod_rows(x, gs, sh):
    return x * lax.rsqrt(jnp.mean(x * x, axis=-1, keepdims=True) + EPS) * gs + sh


NORM_ROWS = 16


def _row_steps(n_rows, body):
    def step(i, carry):
        body(pl.ds(pl.multiple_of(i * NORM_ROWS, NORM_ROWS), NORM_ROWS))
        return carry

    lax.fori_loop(0, n_rows // NORM_ROWS, step, 0, unroll=4)


def _prenorm_kernel(x_ref, g_ref, sh_ref, sc_ref, o_ref):
    gs, sh = g_ref[...] * (1.0 + sc_ref[...]), sh_ref[...]

    def body(rows):
        o_ref[rows, :] = _norm_mod_rows(x_ref[rows, :], gs, sh).astype(BF16)

    _row_steps(x_ref.shape[0], body)


def _prenorm(x, g, sh, sc):
    b, t, d = x.shape
    tr = _tile(t, 256, SUBLANES)
    per_batch = sh.shape[0] > 1
    mod_map = (lambda bb, i: (bb, 0, 0)) if per_batch else (lambda bb, i: (0, 0, 0))
    row = pl.BlockSpec((None, tr, d), lambda bb, i: (bb, i, 0))
    return pl.pallas_call(
        _prenorm_kernel,
        out_shape=jax.ShapeDtypeStruct((b, t, d), BF16),
        grid=(b, t // tr),
        in_specs=[row, pl.BlockSpec((1, d), lambda bb, i: (0, 0)),
                  pl.BlockSpec((None, 1, d), mod_map), pl.BlockSpec((None, 1, d), mod_map)],
        out_specs=row,
        compiler_params=_cparams(("parallel", "parallel")),
        name="prenorm",
    )(x, g, sh, sc)


def _inproj_kernel(h_ref, w_ref, wba_ref, o_ref, oba_ref):
    @pl.when(pl.program_id(2) == 0)
    def _():
        oba_ref[...] = jnp.dot(h_ref[...], wba_ref[...], preferred_element_type=F32)

    o_ref[...] = jnp.dot(h_ref[...], w_ref[...], preferred_element_type=F32)


def _inproj(h, w, n, w_ba):
    b, t, d = h.shape
    nba = w_ba.shape[1]
    tm = _tile(t, 1024, SUBLANES)
    tn = _tile(n, 768 if tm > 512 else 1536)
    return pl.pallas_call(
        _inproj_kernel,
        out_shape=(jax.ShapeDtypeStruct((b, t, n), F32), jax.ShapeDtypeStruct((b, t, nba), F32)),
        grid=(b, t // tm, n // tn),
        in_specs=[pl.BlockSpec((None, tm, d), lambda bb, i, j: (bb, i, 0)),
                  pl.BlockSpec((d, tn), lambda bb, i, j: (0, j)),
                  pl.BlockSpec((d, nba), lambda bb, i, j: (0, 0))],
        out_specs=(pl.BlockSpec((None, tm, tn), lambda bb, i, j: (bb, i, j)),
                   pl.BlockSpec((None, tm, nba), lambda bb, i, j: (bb, i, 0))),
        compiler_params=_cparams(("parallel", "parallel", "arbitrary")),
        name="inproj",
    )(h, w, w_ba)


def _fill_padded(xp_ref, x_ref, n, pad):
    zeros = jnp.zeros((pad, xp_ref.shape[1]), F32)
    xp_ref[0:pad, :] = zeros
    xp_ref[pad + n:pad + n + pad, :] = zeros
    step = min(n, 512)

    def body(i, carry):
        rows = pl.ds(pl.multiple_of(i * step, step), step)
        xp_ref[pl.ds(pl.multiple_of(pad + i * step, SUBLANES), step), :] = x_ref[rows, :]
        return carry

    lax.fori_loop(0, n // step, body, 0)


def _row_pitch(stride):
    return stride if stride == 1 else stride + SUBLANES


def _conv_rows(xp_ref, cw_ref, t0, rows, pad, stride):
    acc = None
    for j in range(CONV_W):
        off = pad + (j - 2) * stride
        term = cw_ref[j:j + 1, :] * xp_ref[pl.ds(t0 + off, rows), :]
        acc = term if acc is None else acc + term
    return acc


LRU_ROWS = 2048
LRU_PAD = 8


def _lru_coeffs(xc, d, wa_ref, ba_ref, wx_ref, bx_ref, lam_ref):
    xb = xc.astype(BF16)
    r = _sigmoid(jnp.dot(xb, wa_ref[d], preferred_element_type=F32) + ba_ref[d:d + 1, :])
    i = _sigmoid(jnp.dot(xb, wx_ref[d], preferred_element_type=F32) + bx_ref[d:d + 1, :])
    log_a = r * ((-LRU_C) * _softplus(-lam_ref[d:d + 1, :]))
    a = jnp.exp(log_a)
    th = jnp.tanh(log_a)
    y = -2.0 * th / (1.0 - th)
    return a, (y * lax.rsqrt(jnp.maximum(y, F32_TINY))) * (i * xc)


def _lru_scan_chunk(a, b, h, rev):
    rows, c = a.shape
    nt = rows // SUBLANES
    a3 = a.reshape(nt, SUBLANES, c)
    b3 = b.reshape(nt, SUBLANES, c)
    ridx = lax.broadcasted_iota(jnp.int32, (nt, SUBLANES, c), 1)
    for s in (1, 2, 4):
        shift = (SUBLANES - s) if rev else s
        a_s = pltpu.roll(a3, shift, 1)
        b_s = pltpu.roll(b3, shift, 1)
        m = (ridx < SUBLANES - s) if rev else (ridx >= s)
        b3 = jnp.where(m, a3 * b_s + b3, b3)
        a3 = jnp.where(m, a3 * a_s, a3)
    out = [None] * nt
    order = range(nt - 1, -1, -1) if rev else range(nt)
    for t in order:
        ht = b3[t] + a3[t] * h
        out[t] = ht
        h = ht[0:1, :] if rev else ht[SUBLANES - 1:SUBLANES, :]
    return jnp.concatenate(out, axis=0), h


def _lru_kernel(ul_ref, gl_ref, uc_ref, cw_ref, cb_ref, wa_ref, ba_ref, wx_ref, bx_ref, lam_ref,
                o_ref, xpl_ref, xpc_ref, xc_ref, hf_ref):
    t_lat = ul_ref.shape[0]
    t_ctx = uc_ref.shape[0]
    c = ul_ref.shape[1]
    rl = min(LRU_ROWS, t_lat)
    rc = min(LRU_ROWS, t_ctx)
    _fill_padded(xpl_ref, ul_ref, t_lat, LRU_PAD)
    _fill_padded(xpc_ref, uc_ref, t_ctx, LRU_PAD)
    cb = cb_ref[...]

    for d in range(2):
        rev = d == 1
        prm = (wa_ref, ba_ref, wx_ref, bx_ref, lam_ref)

        def ctx_body(i, h, rev=rev, d=d, prm=prm):
            blk = (t_ctx // rc - 1 - i) if rev else i
            t0 = pl.multiple_of(blk * rc, rc)
            xc = _conv_rows(xpc_ref, cw_ref, t0, rc, LRU_PAD, 1) + cb
            a, b = _lru_coeffs(xc, d, *prm)
            _, h = _lru_scan_chunk(a, b, h, rev)
            return h

        h = lax.fori_loop(0, t_ctx // rc, ctx_body, jnp.zeros((1, c), F32))

        def lat_body(i, h, rev=rev, d=d, prm=prm):
            blk = (t_lat // rl - 1 - i) if rev else i
            t0 = pl.multiple_of(blk * rl, rl)
            rows = pl.ds(t0, rl)
            if rev:
                xc = xc_ref[rows, :]
            else:
                xc = _conv_rows(xpl_ref, cw_ref, t0, rl, LRU_PAD, 1) + cb
                xc_ref[rows, :] = xc
            a, b = _lru_coeffs(xc, d, *prm)
            hs, h = _lru_scan_chunk(a, b, h, rev)
            if rev:
                y = (hf_ref[rows, :] + hs) * jax.nn.gelu(gl_ref[rows, :], approximate=True)
                o_ref[rows, :] = y.astype(BF16)
            else:
                hf_ref[rows, :] = hs
            return h

        lax.fori_loop(0, t_lat // rl, lat_body, h)


def _lru(p_lat, p_ctx, conv_w, conv_b, w_a, b_a, w_x, b_x, lam, lw):
    b, t, _ = p_lat.shape
    tc = p_ctx.shape[1]
    nb, bd = w_a.shape[1], w_a.shape[2]
    assert bd == LANES and nb * bd == lw
    vec = lambda bb, n: (0, n)
    return pl.pallas_call(
        _lru_kernel,
        out_shape=jax.ShapeDtypeStruct((b, t, lw), BF16),
        grid=(b, nb),
        in_specs=[pl.BlockSpec((None, t, bd), lambda bb, n: (bb, 0, n)),
                  pl.BlockSpec((None, t, bd), lambda bb, n: (bb, 0, nb + n)),
                  pl.BlockSpec((None, tc, bd), lambda bb, n: (bb, 0, n)),
                  pl.BlockSpec((CONV_W, bd), vec),
                  pl.BlockSpec((1, bd), vec),
                  pl.BlockSpec((2, None, bd, bd), lambda bb, n: (0, n, 0, 0)),
                  pl.BlockSpec((2, bd), vec),
                  pl.BlockSpec((2, None, bd, bd), lambda bb, n: (0, n, 0, 0)),
                  pl.BlockSpec((2, bd), vec),
                  pl.BlockSpec((2, bd), vec)],
        out_specs=pl.BlockSpec((None, t, bd), lambda bb, n: (bb, 0, n)),
        scratch_shapes=[pltpu.VMEM((t + 2 * LRU_PAD, bd), F32),
                        pltpu.VMEM((tc + 2 * LRU_PAD, bd), F32),
                        pltpu.VMEM((t, bd), F32),
                        pltpu.VMEM((t, bd), F32)],
        compiler_params=_cparams(("parallel", "parallel")),
        name="lru",
    )(p_lat, p_lat, p_ctx, conv_w, conv_b, w_a, b_a, w_x, b_x, lam)


def _gdnpre_kernel(stride, n_heads, x_ref, cw_ref, o_ref, xp_ref, y_ref):
    t = x_ref.shape[0]
    pad = max(2 * stride, SUBLANES)
    _fill_padded(xp_ref, x_ref, t, pad)
    step = min(t, 512)
    pitch = _row_pitch(stride)
    per = step // stride if stride > 1 else 1

    def conv_body(i, carry):
        t0 = pl.multiple_of(i * step, step)
        y = _conv_rows(xp_ref, cw_ref, t0, step, pad, stride)
        if stride == 1:
            y_ref[pl.ds(t0, step), :] = y
        else:
            for j in range(per):
                dst = pl.multiple_of((i * per + j) * pitch, SUBLANES)
                y_ref[pl.ds(dst, stride), :] = y[j * stride:(j + 1) * stride, :]
        return carry

    lax.fori_loop(0, t // step, conv_body, 0)

    if stride > 1:
        w = stride
        n_rows = t // w
        last = t - w
        ylast = (n_rows - 1) * pitch
        col = lax.broadcasted_iota(jnp.int32, (w, 1), 0)
        not_first = col >= 1
        not_last = col <= w - 2
        prev_bot = xp_ref[pl.ds(pad + last - 1, w), :]
        prev_bot2 = xp_ref[pl.ds(pad + last - w - 1, w), :]
        next_top = xp_ref[pl.ds(pad + 1, w), :]
        y_ref[0:w, :] = y_ref[0:w, :] + jnp.where(
            not_first, cw_ref[1:2, :] * prev_bot + cw_ref[0:1, :] * prev_bot2, 0.0)
        y_ref[pitch:pitch + w, :] = y_ref[pitch:pitch + w, :] + jnp.where(
            not_first, cw_ref[0:1, :] * prev_bot, 0.0)
        y_ref[ylast:ylast + w, :] = y_ref[ylast:ylast + w, :] + jnp.where(
            not_last, cw_ref[3:4, :] * next_top, 0.0)

    blk = pl.program_id(1)
    hd = x_ref.shape[1]

    def finish(normed, scale):
        def act(y):
            y = _silu(y)
            if normed:
                y = y * (lax.rsqrt(jnp.sum(y * y, axis=-1, keepdims=True) + EPS) * scale)
            return y.astype(BF16)

        if stride == 1:
            def body(i, carry):
                rows = pl.ds(pl.multiple_of(i * step, step), step)
                o_ref[rows, :] = act(y_ref[rows, :])
                return carry

            lax.fori_loop(0, t // step, body, 0)
        else:
            n_rows = t // stride

            def body(w, carry):
                dst = pl.ds(pl.multiple_of(w * n_rows, n_rows), n_rows)
                o_ref[dst, :] = act(y_ref[pl.ds(w, n_rows, stride=pitch), :])
                return carry

            lax.fori_loop(0, stride, body, 0, unroll=8)

    @pl.when(blk < n_heads)
    def _():
        finish(True, hd ** -0.5)

    @pl.when(jnp.logical_and(blk >= n_heads, blk < 2 * n_heads))
    def _():
        finish(True, 1.0)

    @pl.when(blk >= 2 * n_heads)
    def _():
        finish(False, 1.0)


def _gdnpre(p, col0, conv_w, n_heads, stride):
    b, t, _ = p.shape
    hd = LANES
    nblk = 3 * n_heads
    assert col0 % hd == 0 and t % stride == 0
    pad = max(2 * stride, SUBLANES)
    y_rows = (t // stride) * _row_pitch(stride)
    return pl.pallas_call(
        functools.partial(_gdnpre_kernel, stride, n_heads),
        out_shape=jax.ShapeDtypeStruct((b, t, nblk * hd), BF16),
        grid=(b, nblk),
        in_specs=[pl.BlockSpec((None, t, hd), lambda bb, n: (bb, 0, col0 // hd + n)),
                  pl.BlockSpec((CONV_W, hd), lambda bb, n: (0, n))],
        out_specs=pl.BlockSpec((None, t, hd), lambda bb, n: (bb, 0, n)),
        scratch_shapes=[pltpu.VMEM((t + 2 * pad, hd), F32), pltpu.VMEM((y_rows, hd), F32)],
        compiler_params=_cparams(("parallel", "parallel")),
        name="gdnpre",
    )(p, conv_w)


GDN_GROUP_ITEMS = 32


def _split3(x):
    hi = x.astype(BF16)
    r1 = x - hi.astype(F32)
    mid = r1.astype(BF16)
    lo = (r1 - mid.astype(F32)).astype(BF16)
    return hi, mid, lo


def _gdn_kernel(mode, rev, d, gh, n_cast, *refs):
    n_in = {"ctx": 5, "fwd": 6, "bwd": 7}[mode]
    cast_in = refs[n_in:n_in + n_cast]
    cast_out = refs[n_in + n_cast + 1:n_in + 2 * n_cast + 1]
    refs = refs[:n_in] + refs[n_in + n_cast:n_in + n_cast + 1] + refs[n_in + 2 * n_cast + 1:]
    if mode == "ctx":
        q_ref, k_ref, v_ref, ba_ref, prm_ref, sfin_ref, s_scr = refs
    elif mode == "fwd":
        q_ref, k_ref, v_ref, ba_ref, prm_ref, s0_ref, o_ref, s_scr = refs
    else:
        q_ref, k_ref, v_ref, ba_ref, prm_ref, s0_ref, of_ref, o_ref, s_scr = refs
    for src, dst in zip(cast_in, cast_out):
        dst[...] = src[...].astype(BF16)
    r = q_ref.shape[0]
    hd = LANES
    nch = r // CHUNK
    step = pl.program_id(1)
    nsteps = pl.num_programs(1)

    @pl.when(step == 0)
    def _():
        if mode == "ctx":
            s_scr[...] = jnp.zeros(s_scr.shape, F32)
        else:
            s_scr[...] = s0_ref[...]

    ba = ba_ref[...]
    sig = _sigmoid(ba)
    gfull = -jnp.exp(prm_ref[0:1, :]) * _softplus(ba + prm_ref[1:2, :])
    ii = lax.broadcasted_iota(jnp.int32, (r, r), 0)
    jj = lax.broadcasted_iota(jnp.int32, (r, r), 1)
    same = (ii // CHUNK) == (jj // CHUNK)
    tri = jnp.where(jnp.logical_and(same, (jj >= ii) if rev else (jj <= ii)), 1.0, 0.0).astype(BF16)
    g_hi, g_mid, g_lo = _split3(gfull)
    gc = (jnp.dot(tri, g_hi, preferred_element_type=F32)
          + jnp.dot(tri, g_mid, preferred_element_type=F32)
          + jnp.dot(tri, g_lo, preferred_element_type=F32))
    sig_t = sig.T
    gc_t = gc.T

    ci = lax.broadcasted_iota(jnp.int32, (CHUNK, CHUNK), 0)
    cj = lax.broadcasted_iota(jnp.int32, (CHUNK, CHUNK), 1)
    low = (ci <= cj) if rev else (ci >= cj)
    strict = (ci < cj) if rev else (ci > cj)
    eye = jnp.where(ci == cj, 1.0, 0.0).astype(F32)
    last = 0 if rev else CHUNK - 1
    chunk_order = list(range(nch - 1, -1, -1) if rev else range(nch))
    heads_per_group = max(1, min(gh, GDN_GROUP_ITEMS // nch))

    for g0 in range(0, gh, heads_per_group):
        heads = list(range(g0, min(gh, g0 + heads_per_group)))
        items = [(h, c) for h in heads for c in range(nch)]
        st = {}
        for (h, c) in items:
            rows = slice(c * CHUNK, (c + 1) * CHUNK)
            lanes = slice(h * hd, (h + 1) * hd)
            cb = d * gh + h
            cg = 2 * gh + d * gh + h
            bcol = sig[rows, cb:cb + 1]
            gcol = jnp.broadcast_to(gc[rows, cg:cg + 1], (CHUNK, hd))
            grow = gc_t[cg:cg + 1, rows]
            brow = sig_t[cb:cb + 1, rows]
            kc = k_ref[rows, lanes]
            decay = jnp.where(low, jnp.exp(jnp.where(low, gcol[:, :CHUNK] - grow, 0.0)), 0.0)
            x = -jnp.where(strict, _mm_nt(kc, kc) * bcol * decay, 0.0)
            glast = gcol[last:last + 1, :]
            it = dict(rows=rows, lanes=lanes, kc=kc, brow=brow, grow=grow, egl=jnp.exp(glast),
                      egcol=jnp.exp(gcol), etail=jnp.exp(glast - gcol),
                      xb=x.astype(BF16), t=eye + x)
            if mode != "ctx":
                it["intra"] = (_mm_nt(q_ref[rows, lanes], kc) * decay).astype(BF16)
            st[(h, c)] = it
        for _ in range(5):
            for it in st.values():
                it["xb"] = jnp.dot(it["xb"], it["xb"], preferred_element_type=F32).astype(BF16)
            for it in st.values():
                it["t"] = it["t"] + jnp.dot(it["t"].astype(BF16), it["xb"],
                                            preferred_element_type=F32)
        for it in st.values():
            tb = it["t"] * it["brow"]
            it["u"] = _mm(tb, v_ref[it["rows"], it["lanes"]])
            it["w"] = _mm(tb * jnp.exp(it["grow"]), it["kc"]).astype(BF16)
            del it["t"], it["xb"]
        state = {h: s_scr[h] for h in heads}
        for c in chunk_order:
            cur = [st[(h, c)] for h in heads]
            sb = [state[h].astype(BF16) for h in heads]
            ws = [jnp.dot(it["w"], s, preferred_element_type=F32) for it, s in zip(cur, sb)]
            if mode != "ctx":
                qs = [jnp.dot(q_ref[it["rows"], it["lanes"]], s, preferred_element_type=F32)
                      for it, s in zip(cur, sb)]
            vnew = [it["u"] - x for it, x in zip(cur, ws)]
            if mode != "ctx":
                iv = [_mm(it["intra"], v) for it, v in zip(cur, vnew)]
            kv = [_mm_tn(it["kc"], v * it["etail"]) for it, v in zip(cur, vnew)]
            for i, h in enumerate(heads):
                it = cur[i]
                if mode != "ctx":
                    o = it["egcol"] * qs[i] + iv[i]
                    if mode == "bwd":
                        o = o + of_ref[it["rows"], it["lanes"]]
                    o_ref[it["rows"], it["lanes"]] = o
                state[h] = state[h] * it["egl"] + kv[i]
        for h in heads:
            s_scr[h] = state[h]

    if mode == "ctx":
        @pl.when(step == nsteps - 1)
        def _():
            sfin_ref[...] = s_scr[...]


def _gdn_call(mode, rev, d, gh, qkv, ba, prm, r, s0=None, o_f=None, casts=()):
    b, t, _ = qkv.shape
    hd = LANES
    gw = gh * hd
    n_blk = t // r
    assert t % r == 0 and r % CHUNK == 0 and r % LANES == 0
    blk = (lambda w: n_blk - 1 - w) if rev else (lambda w: w)
    in_specs = [pl.BlockSpec((None, r, gw), lambda bb, w: (bb, blk(w), 0)),
                pl.BlockSpec((None, r, gw), lambda bb, w: (bb, blk(w), 1)),
                pl.BlockSpec((None, r, gw), lambda bb, w: (bb, blk(w), 2)),
                pl.BlockSpec((None, r, LANES), lambda bb, w: (bb, blk(w), 0)),
                pl.BlockSpec((SUBLANES, LANES), lambda bb, w: (0, 0))]
    args = [qkv, qkv, qkv, ba, prm]
    state_spec = pl.BlockSpec((None, gh, hd, hd), lambda bb, w: (bb, 0, 0, 0))
    o_spec = pl.BlockSpec((None, r, gw), lambda bb, w: (bb, blk(w), 0))
    if mode != "ctx":
        in_specs.append(state_spec)
        args.append(s0)
    if mode == "bwd":
        in_specs.append(o_spec)
        args.append(o_f)
    if mode == "ctx":
        out_shape = [jax.ShapeDtypeStruct((b, gh, hd, hd), F32)]
        out_specs = [state_spec]
    else:
        out_shape = [jax.ShapeDtypeStruct((b, t, gw), F32)]
        out_specs = [o_spec]
    n_steps = b * n_blk
    for wmat in casts:
        nr, nc = wmat.shape
        rp = _cast_rows(nr, n_steps)
        last = nr // rp - 1
        spec = pl.BlockSpec((rp, nc), lambda bb, w, last=last: (jnp.minimum(bb * n_blk + w, last), 0))
        in_specs.append(spec)
        args.append(wmat)
        out_shape.append(jax.ShapeDtypeStruct((nr, nc), BF16))
        out_specs.append(spec)
    outs = pl.pallas_call(
        functools.partial(_gdn_kernel, mode, rev, d, gh, len(casts)),
        out_shape=tuple(out_shape),
        grid=(b, n_blk),
        in_specs=in_specs,
        out_specs=tuple(out_specs),
        scratch_shapes=[pltpu.VMEM((gh, hd, hd), F32)],
        compiler_params=_cparams(("arbitrary", "arbitrary")),
        name="gdn_%s_%d" % (mode, d),
    )(*args)
    return outs[0] if not casts else outs


def _cast_rows(n_rows, n_steps):
    for rp in range(2 * SUBLANES, n_rows, 2 * SUBLANES):
        if n_rows % rp == 0 and n_rows // rp <= n_steps:
            return rp
    return n_rows


def _to_visit_kernel(width, x_ref, o_ref):
    n_rows = x_ref.shape[0] // width

    def body(w, carry):
        dst = pl.ds(pl.multiple_of(w * n_rows, n_rows), n_rows)
        o_ref[dst, :] = x_ref[pl.ds(w, n_rows, stride=width), :]
        return carry

    lax.fori_loop(0, width, body, 0)


def _to_visit(x, width):
    b, t, c = x.shape
    assert c == LANES
    spec = pl.BlockSpec((None, t, c), lambda bb: (bb, 0, 0))
    return pl.pallas_call(
        functools.partial(_to_visit_kernel, width),
        out_shape=jax.ShapeDtypeStruct((b, t, c), F32),
        grid=(b,),
        in_specs=[spec],
        out_specs=spec,
        compiler_params=_cparams(("parallel",)),
        name="to_visit",
    )(x)


def _gdnpost_kernel(width, o_ref, z_ref, ng_ref, y_ref, scr_ref):
    t = o_ref.shape[0]
    n_rows = t // width
    pitch = _row_pitch(width)

    def scatter(w, carry):
        src = pl.ds(pl.multiple_of(w * n_rows, n_rows), n_rows)
        scr_ref[pl.ds(w, n_rows, stride=pitch), :] = o_ref[src, :]
        return carry

    lax.fori_loop(0, width, scatter, 0)
    per = SUBLANES
    ng = ng_ref[...]

    def body(i, carry):
        parts = [scr_ref[pl.ds(pl.multiple_of((i * per + j) * pitch, SUBLANES), width), :]
                 for j in range(per)]
        o = jnp.concatenate(parts, axis=0)
        rows = pl.ds(pl.multiple_of(i * per * width, per * width), per * width)
        y = o * lax.rsqrt(jnp.mean(o * o, axis=-1, keepdims=True) + EPS) * ng
        y_ref[rows, :] = (y * _silu(z_ref[rows, :])).astype(BF16)
        return carry

    lax.fori_loop(0, n_rows // per, body, 0)


def _gdnpost(o, p, zcol0, ng, gh, width):
    b, t, gw = o.shape
    hd = LANES
    n_rows = t // width
    assert zcol0 % hd == 0 and n_rows % SUBLANES == 0
    return pl.pallas_call(
        functools.partial(_gdnpost_kernel, width),
        out_shape=jax.ShapeDtypeStruct((b, t, gw), BF16),
        grid=(b, gh),
        in_specs=[pl.BlockSpec((None, t, hd), lambda bb, h: (bb, 0, h)),
                  pl.BlockSpec((None, t, hd), lambda bb, h: (bb, 0, zcol0 // hd + h)),
                  pl.BlockSpec((1, hd), lambda bb, h: (0, 0))],
        out_specs=pl.BlockSpec((None, t, hd), lambda bb, h: (bb, 0, h)),
        scratch_shapes=[pltpu.VMEM((n_rows * _row_pitch(width), hd), F32)],
        compiler_params=_cparams(("parallel", "parallel")),
        name="gdnpost",
    )(o, p, ng)


def _outproj_kernel(gl_ref, yg_ref, ng_ref, w_ref, o_ref, a_scr):
    tm, lw = gl_ref.shape

    @pl.when(pl.program_id(1) == 0)
    def _():
        ng = ng_ref[...]

        def body(r, carry):
            rows = pl.ds(pl.multiple_of(r * ROW_CHUNK, ROW_CHUNK), ROW_CHUNK)
            x = gl_ref[rows, :].astype(F32)
            y = x * lax.rsqrt(jnp.mean(x * x, axis=-1, keepdims=True) + EPS) * ng
            a_scr[rows, 0:lw] = y.astype(BF16)
            return carry

        lax.fori_loop(0, tm // ROW_CHUNK, body, 0)
        a_scr[:, lw:] = yg_ref[...]

    o_ref[...] = jnp.dot(a_scr[...], w_ref[...], preferred_element_type=F32).astype(o_ref.dtype)


def _outproj(g_lru, y_gdn, ng, w_out):
    m, lw = g_lru.shape
    gw = y_gdn.shape[1]
    k, n = w_out.shape
    tm = _tile(m, 1024, SUBLANES)
    tn = _tile(n, 512)
    return pl.pallas_call(
        _outproj_kernel,
        out_shape=jax.ShapeDtypeStruct((m, n), BF16),
        grid=(m // tm, n // tn),
        in_specs=[pl.BlockSpec((tm, lw), lambda i, j: (i, 0)),
                  pl.BlockSpec((tm, gw), lambda i, j: (i, 0)),
                  pl.BlockSpec((1, lw), lambda i, j: (0, 0)),
                  pl.BlockSpec((k, tn), lambda i, j: (0, j))],
        out_specs=pl.BlockSpec((tm, tn), lambda i, j: (i, j)),
        scratch_shapes=[pltpu.VMEM((tm, k), BF16)],
        compiler_params=_cparams(("parallel", "arbitrary")),
        name="outproj",
    )(g_lru, y_gdn, ng, w_out)


def _residual(x, y, gpg):
    y = y.astype(F32)
    return x + y * lax.rsqrt(jnp.mean(y * y, axis=-1, keepdims=True) + EPS) * gpg


def _post_kernel(final, *refs):
    if final:
        x_ref, m_ref, gpm_ref, gatem_ref, y_ref, gpf_ref, gatef_ref, o_ref = refs
        gpg_f = gpf_ref[...] * gatef_ref[...]
    else:
        x_ref, m_ref, gpm_ref, gatem_ref, gn_ref, sh_ref, sc_ref, o_ref = refs
        gs, sh = gn_ref[...] * (1.0 + sc_ref[...]), sh_ref[...]
    gpg_m = gpm_ref[...] * gatem_ref[...]

    def body(rows):
        x1 = _residual(x_ref[rows, :], m_ref[rows, :], gpg_m)
        if final:
            o_ref[rows, :] = _residual(x1, y_ref[rows, :], gpg_f)
        else:
            o_ref[rows, :] = _norm_mod_rows(x1, gs, sh).astype(BF16)

    _row_steps(x_ref.shape[0], body)


def _post(x, mix, g_post_mix, gate_m, tail, final):
    b, t, d = x.shape
    tr = _tile(t, 256, SUBLANES)
    row = pl.BlockSpec((None, tr, d), lambda bb, i: (bb, i, 0))
    vec = pl.BlockSpec((1, d), lambda bb, i: (0, 0))
    mod = pl.BlockSpec((None, 1, d), lambda bb, i: (bb, 0, 0))
    return pl.pallas_call(
        functools.partial(_post_kernel, final),
        out_shape=jax.ShapeDtypeStruct((b, t, d), F32 if final else BF16),
        grid=(b, t // tr),
        in_specs=[row, row, vec, mod] + ([row, vec, mod] if final else [vec, mod, mod]),
        out_specs=row,
        compiler_params=_cparams(("parallel", "parallel")),
        name="post" if final else "post_next",
    )(x, mix, g_post_mix, gate_m, *tail)


def _ffn_kernel(overlap, h_ref, wg_ref, wu_ref, wd_ref, o_ref, acc_ref):
    f = pl.program_id(1)
    last = pl.num_programs(1) - 1

    @pl.when(f == 0)
    def _():
        acc_ref[...] = jnp.zeros(acc_ref.shape, F32)

    h = h_ref[...]
    g = jnp.dot(h, wg_ref[...], preferred_element_type=F32)
    u = jnp.dot(h, wu_ref[...], preferred_element_type=F32)
    act = _silu(g) * u
    if overlap:
        col = lax.broadcasted_iota(jnp.int32, act.shape, 1)
        act = jnp.where(col >= jnp.where(f == last, overlap, 0), act, 0.0)
    acc_ref[...] += jnp.dot(act.astype(BF16), wd_ref[...], preferred_element_type=F32)

    @pl.when(f == last)
    def _():
        o_ref[...] = acc_ref[...].astype(o_ref.dtype)


FFN_HIDDEN_TILE = 512


def _ffn(h, wg, wu, wd):
    m, d = h.shape
    dff = wg.shape[1]
    tm = _tile(m, 512, SUBLANES)
    tf = min(FFN_HIDDEN_TILE, dff)
    n_f = pl.cdiv(dff, tf)
    overlap = n_f * tf - dff
    assert overlap % LANES == 0
    assert tf % LANES == 0 and dff % LANES == 0
    start = lambda f: jnp.minimum(f * (tf // LANES), (dff - tf) // LANES) * LANES
    return pl.pallas_call(
        functools.partial(_ffn_kernel, overlap),
        out_shape=jax.ShapeDtypeStruct((m, d), BF16),
        grid=(m // tm, n_f),
        in_specs=[pl.BlockSpec((tm, d), lambda i, f: (i, 0)),
                  pl.BlockSpec((pl.Element(d), pl.Element(tf)), lambda i, f: (0, start(f))),
                  pl.BlockSpec((pl.Element(d), pl.Element(tf)), lambda i, f: (0, start(f))),
                  pl.BlockSpec((pl.Element(tf), pl.Element(d)), lambda i, f: (start(f), 0))],
        out_specs=pl.BlockSpec((tm, d), lambda i, f: (i, 0)),
        scratch_shapes=[pltpu.VMEM((tm, d), F32)],
        compiler_params=_cparams(("parallel", "arbitrary")),
        name="ffn",
    )(h, wg, wu, wd)


def _pad_cols(w, mult):
    n = w.shape[-1]
    padn = (-n) % mult
    return w if padn == 0 else jnp.pad(w, ((0, 0), (0, padn)))


def kernel(x, c, ctx, c_ctx, w_ada, b_ada, g_pre_mix, g_post_mix, g_pre_ffn, g_post_ffn, w_in,
           lru_conv_w, lru_conv_b, lru_w_a, lru_b_a, lru_w_x, lru_b_x, lru_lambda, lru_norm_g,
           gdn_conv_w, gdn_a_log, gdn_dt_bias, gdn_norm_g, w_out, w_ffn_gate, w_ffn_up, w_ffn_down):
    depth = w_ada.shape[0]
    bsz, seq, d = x.shape
    tctx = ctx.shape[1]
    lw = lru_lambda.shape[-1]
    gh = gdn_a_log.shape[-1]
    hd = gdn_norm_g.shape[-1]
    gw = gh * hd
    n_main = 2 * lw + 4 * gw
    rows = seq // GRID_W
    assert hd == LANES and 4 * gh <= LANES and seq == rows * GRID_W
    assert n_main % gw == 0 and (2 * lw + 3 * gw) % gw == 0

    for l in range(depth):
        assert l == depth - 1, "context stream updates (depth > 1) are not implemented"

        crows = jnp.zeros((SUBLANES, d), F32).at[0:bsz].set(c).at[bsz].set(c_ctx)
        mod = _ada(crows, w_ada[l], b_ada[l][None, :])
        mod_lat = mod[0:bsz].reshape(bsz, N_MOD, 1, d)
        shift_m, scale_m, gate_m, shift_f, scale_f, gate_f = (mod_lat[:, i] for i in range(N_MOD))
        mod_ctx = mod[bsz].reshape(N_MOD, 1, 1, d)

        w_all = w_in[l].astype(BF16)
        w_ba = _pad_cols(w_in[l][:, n_main:], LANES).astype(BF16)
        g_pre = g_pre_mix[l][None, :]
        p_lat, ba_lat = _inproj(_prenorm(x, g_pre, shift_m, scale_m), w_all, n_main, w_ba)
        h_ctx = _prenorm(ctx, g_pre, mod_ctx[0], mod_ctx[1]).reshape(1, bsz * tctx, d)
        p_ctx, ba_ctx = _inproj(h_ctx, w_all, n_main, w_ba)
        p_ctx = p_ctx.reshape(bsz, tctx, n_main)
        ba_ctx = ba_ctx.reshape(bsz, tctx, LANES)

        g_lru = _lru(p_lat, p_ctx, lru_conv_w[l], lru_conv_b[l][None, :], lru_w_a[l].astype(BF16),
                     lru_b_a[l], lru_w_x[l].astype(BF16), lru_b_x[l], lru_lambda[l], lw)

        qkv_lat = _gdnpre(p_lat, 2 * lw, gdn_conv_w[l], gh, GRID_W)
        qkv_ctx = _gdnpre(p_ctx, 2 * lw, gdn_conv_w[l], gh, 1)
        prm = jnp.zeros((SUBLANES, LANES), F32)
        prm = prm.at[0, 2 * gh:4 * gh].set(gdn_a_log[l].reshape(-1))
        prm = prm.at[1, 2 * gh:4 * gh].set(gdn_dt_bias[l].reshape(-1))
        ba_vis = _to_visit(ba_lat, GRID_W)
        s_f = _gdn_call("ctx", False, 0, gh, qkv_ctx, ba_ctx, prm, tctx)
        s_b = _gdn_call("ctx", True, 1, gh, qkv_ctx, ba_ctx, prm, tctx)
        o_f, wg_b, wu_b = _gdn_call("fwd", False, 0, gh, qkv_lat, ba_vis, prm, rows, s0=s_f,
                                    casts=(w_ffn_gate[l], w_ffn_up[l]))
        o_fb, wd_b, wo_b = _gdn_call("bwd", True, 1, gh, qkv_lat, ba_vis, prm, rows, s0=s_b, o_f=o_f,
                                     casts=(w_ffn_down[l], w_out[l]))
        y_gdn = _gdnpost(o_fb, p_lat, 2 * lw + 3 * gw, gdn_norm_g[l][None, :], gh, GRID_W)
        y_gdn = y_gdn.reshape(bsz * seq, gw)

        mix = _outproj(g_lru.reshape(bsz * seq, lw), y_gdn, lru_norm_g[l][None, :], wo_b)
        mix = mix.reshape(bsz, seq, d)
        gpm = g_post_mix[l][None, :]
        h_ffn = _post(x, mix, gpm, gate_m, (g_pre_ffn[l][None, :], shift_f, scale_f), final=False)
        y_ffn = _ffn(h_ffn.reshape(bsz * seq, d), wg_b, wu_b, wd_b)
        x = _post(x, mix, gpm, gate_m, (y_ffn.reshape(bsz, seq, d), g_post_ffn[l][None, :], gate_f),
                  final=True)
    return x
```

```python
import functools

import jax
import jax.numpy as jnp
from jax import lax
from jax.experimental import pallas as pl
from jax.experimental.pallas import tpu as pltpu

F32 = jnp.float32
BF16 = jnp.bfloat16

EPS = 1e-6
F32_TINY = 1.1754944e-38
GRID_W = 64
CHUNK = 64
LRU_C = 8.0
CONV_W = 4
N_MOD = 6
LANES = 128
SUBLANES = 8
VMEM_LIMIT = 56 * 1024 * 1024


def _cparams(sem):
    return pltpu.CompilerParams(dimension_semantics=sem, vmem_limit_bytes=VMEM_LIMIT)


def _mm(a, b):
    return jnp.dot(a.astype(BF16), b.astype(BF16), preferred_element_type=F32)


def _mm_nt(a, b):
    return lax.dot_general(a.astype(BF16), b.astype(BF16), (((1,), (1,)), ((), ())),
                           preferred_element_type=F32)


def _mm_tn(a, b):
    return lax.dot_general(a.astype(BF16), b.astype(BF16), (((0,), (0,)), ((), ())),
                           preferred_element_type=F32)


def _sigmoid(x):
    return 1.0 / (1.0 + jnp.exp(-x))


def _silu(x):
    return x * _sigmoid(x)


def _softplus(x):
    return jnp.maximum(x, 0.0) + jnp.log1p(jnp.exp(-jnp.abs(x)))


def _tile(n, target, mult=LANES):
    if n <= target:
        return n
    t = (target // mult) * mult
    while t >= mult:
        if n % t == 0:
            return t
        t -= mult
    return n


def _ada_kernel(c_ref, w_ref, b_ref, o_ref):
    o_ref[...] = _mm(_silu(c_ref[...]), w_ref[...]) + b_ref[...]


def _ada(crows, w, b):
    r, d = crows.shape
    n = w.shape[1]
    tn = _tile(n, 512)
    return pl.pallas_call(
        _ada_kernel,
        out_shape=jax.ShapeDtypeStruct((r, n), F32),
        grid=(n // tn,),
        in_specs=[pl.BlockSpec((r, d), lambda j: (0, 0)),
                  pl.BlockSpec((d, tn), lambda j: (0, j)),
                  pl.BlockSpec((1, tn), lambda j: (0, j))],
        out_specs=pl.BlockSpec((r, tn), lambda j: (0, j)),
        compiler_params=_cparams(("arbitrary",)),
        name="ada",
    )(crows, w, b)


ROW_CHUNK = 64


def _norm_mod_rows(x, gs, sh):
    return x * lax.rsqrt(jnp.mean(x * x, axis=-1, keepdims=True) + EPS) * gs + sh


NORM_ROWS = 16


def _row_steps(n_rows, body):
    def step(i, carry):
        body(pl.ds(pl.multiple_of(i * NORM_ROWS, NORM_ROWS), NORM_ROWS))
        return carry

    lax.fori_loop(0, n_rows // NORM_ROWS, step, 0, unroll=4)


def _prenorm_kernel(x_ref, g_ref, sh_ref, sc_ref, o_ref):
    gs, sh = g_ref[...] * (1.0 + sc_ref[...]), sh_ref[...]

    def body(rows):
        o_ref[rows, :] = _norm_mod_rows(x_ref[rows, :], gs, sh).astype(BF16)

    _row_steps(x_ref.shape[0], body)


def _prenorm(x, g, sh, sc):
    b, t, d = x.shape
    tr = _tile(t, 256, SUBLANES)
    per_batch = sh.shape[0] > 1
    mod_map = (lambda bb, i: (bb, 0, 0)) if per_batch else (lambda bb, i: (0, 0, 0))
    row = pl.BlockSpec((None, tr, d), lambda bb, i: (bb, i, 0))
    return pl.pallas_call(
        _prenorm_kernel,
        out_shape=jax.ShapeDtypeStruct((b, t, d), BF16),
        grid=(b, t // tr),
        in_specs=[row, pl.BlockSpec((1, d), lambda bb, i: (0, 0)),
                  pl.BlockSpec((None, 1, d), mod_map), pl.BlockSpec((None, 1, d), mod_map)],
        out_specs=row,
        compiler_params=_cparams(("parallel", "parallel")),
        name="prenorm",
    )(x, g, sh, sc)


def _inproj_kernel(h_ref, w_ref, wba_ref, o_ref, oba_ref):
    @pl.when(pl.program_id(2) == 0)
    def _():
        oba_ref[...] = jnp.dot(h_ref[...], wba_ref[...], preferred_element_type=F32)

    o_ref[...] = jnp.dot(h_ref[...], w_ref[...], preferred_element_type=F32)


def _inproj(h, w, n, w_ba):
    b, t, d = h.shape
    nba = w_ba.shape[1]
    tm = _tile(t, 1024, SUBLANES)
    tn = _tile(n, 768 if tm > 512 else 1536)
    return pl.pallas_call(
        _inproj_kernel,
        out_shape=(jax.ShapeDtypeStruct((b, t, n), F32), jax.ShapeDtypeStruct((b, t, nba), F32)),
        grid=(b, t // tm, n // tn),
        in_specs=[pl.BlockSpec((None, tm, d), lambda bb, i, j: (bb, i, 0)),
                  pl.BlockSpec((d, tn), lambda bb, i, j: (0, j)),
                  pl.BlockSpec((d, nba), lambda bb, i, j: (0, 0))],
        out_specs=(pl.BlockSpec((None, tm, tn), lambda bb, i, j: (bb, i, j)),
                   pl.BlockSpec((None, tm, nba), lambda bb, i, j: (bb, i, 0))),
        compiler_params=_cparams(("parallel", "parallel", "arbitrary")),
        name="inproj",
    )(h, w, w_ba)


def _fill_padded(xp_ref, x_ref, n, pad):
    zeros = jnp.zeros((pad, xp_ref.shape[1]), F32)
    xp_ref[0:pad, :] = zeros
    xp_ref[pad + n:pad + n + pad, :] = zeros
    step = min(n, 512)

    def body(i, carry):
        rows = pl.ds(pl.multiple_of(i * step, step), step)
        xp_ref[pl.ds(pl.multiple_of(pad + i * step, SUBLANES), step), :] = x_ref[rows, :]
        return carry

    lax.fori_loop(0, n // step, body, 0)


def _row_pitch(stride):
    return stride if stride == 1 else stride + SUBLANES


def _conv_rows(xp_ref, cw_ref, t0, rows, pad, stride):
    acc = None
    for j in range(CONV_W):
        off = pad + (j - 2) * stride
        term = cw_ref[j:j + 1, :] * xp_ref[pl.ds(t0 + off, rows), :]
        acc = term if acc is None else acc + term
    return acc


LRU_ROWS = 2048
LRU_PAD = 8


def _lru_coeffs(xc, d, wa_ref, ba_ref, wx_ref, bx_ref, lam_ref):
    xb = xc.astype(BF16)
    r = _sigmoid(jnp.dot(xb, wa_ref[d], preferred_element_type=F32) + ba_ref[d:d + 1, :])
    i = _sigmoid(jnp.dot(xb, wx_ref[d], preferred_element_type=F32) + bx_ref[d:d + 1, :])
    log_a = r * ((-LRU_C) * _softplus(-lam_ref[d:d + 1, :]))
    a = jnp.exp(log_a)
    th = jnp.tanh(log_a)
    y = -2.0 * th / (1.0 - th)
    return a, (y * lax.rsqrt(jnp.maximum(y, F32_TINY))) * (i * xc)


def _lru_scan_chunk(a, b, h, rev):
    rows, c = a.shape
    nt = rows // SUBLANES
    a3 = a.reshape(nt, SUBLANES, c)
    b3 = b.reshape(nt, SUBLANES, c)
    ridx = lax.broadcasted_iota(jnp.int32, (nt, SUBLANES, c), 1)
    for s in (1, 2, 4):
        shift = (SUBLANES - s) if rev else s
        a_s = pltpu.roll(a3, shift, 1)
        b_s = pltpu.roll(b3, shift, 1)
        m = (ridx < SUBLANES - s) if rev else (ridx >= s)
        b3 = jnp.where(m, a3 * b_s + b3, b3)
        a3 = jnp.where(m, a3 * a_s, a3)
    out = [None] * nt
    order = range(nt - 1, -1, -1) if rev else range(nt)
    for t in order:
        ht = b3[t] + a3[t] * h
        out[t] = ht
        h = ht[0:1, :] if rev else ht[SUBLANES - 1:SUBLANES, :]
    return jnp.concatenate(out, axis=0), h


def _lru_kernel(ul_ref, gl_ref, uc_ref, cw_ref, cb_ref, wa_ref, ba_ref, wx_ref, bx_ref, lam_ref,
                o_ref, xpl_ref, xpc_ref, xc_ref, hf_ref):
    t_lat = ul_ref.shape[0]
    t_ctx = uc_ref.shape[0]
    c = ul_ref.shape[1]
    rl = min(LRU_ROWS, t_lat)
    rc = min(LRU_ROWS, t_ctx)
    _fill_padded(xpl_ref, ul_ref, t_lat, LRU_PAD)
    _fill_padded(xpc_ref, uc_ref, t_ctx, LRU_PAD)
    cb = cb_ref[...]

    for d in range(2):
        rev = d == 1
        prm = (wa_ref, ba_ref, wx_ref, bx_ref, lam_ref)

        def ctx_body(i, h, rev=rev, d=d, prm=prm):
            blk = (t_ctx // rc - 1 - i) if rev else i
            t0 = pl.multiple_of(blk * rc, rc)
            xc = _conv_rows(xpc_ref, cw_ref, t0, rc, LRU_PAD, 1) + cb
            a, b = _lru_coeffs(xc, d, *prm)
            _, h = _lru_scan_chunk(a, b, h, rev)
            return h

        h = lax.fori_loop(0, t_ctx // rc, ctx_body, jnp.zeros((1, c), F32))

        def lat_body(i, h, rev=rev, d=d, prm=prm):
            blk = (t_lat // rl - 1 - i) if rev else i
            t0 = pl.multiple_of(blk * rl, rl)
            rows = pl.ds(t0, rl)
            if rev:
                xc = xc_ref[rows, :]
            else:
                xc = _conv_rows(xpl_ref, cw_ref, t0, rl, LRU_PAD, 1) + cb
                xc_ref[rows, :] = xc
            a, b = _lru_coeffs(xc, d, *prm)
            hs, h = _lru_scan_chunk(a, b, h, rev)
            if rev:
                y = (hf_ref[rows, :] + hs) * jax.nn.gelu(gl_ref[rows, :], approximate=True)
                o_ref[rows, :] = y.astype(BF16)
            else:
                hf_ref[rows, :] = hs
            return h

        lax.fori_loop(0, t_lat // rl, lat_body, h)


def _lru(p_lat, p_ctx, conv_w, conv_b, w_a, b_a, w_x, b_x, lam, lw):
    b, t, _ = p_lat.shape
    tc = p_ctx.shape[1]
    nb, bd = w_a.shape[1], w_a.shape[2]
    assert bd == LANES and nb * bd == lw
    vec = lambda bb, n: (0, n)
    return pl.pallas_call(
        _lru_kernel,
        out_shape=jax.ShapeDtypeStruct((b, t, lw), BF16),
        grid=(b, nb),
        in_specs=[pl.BlockSpec((None, t, bd), lambda bb, n: (bb, 0, n)),
                  pl.BlockSpec((None, t, bd), lambda bb, n: (bb, 0, nb + n)),
                  pl.BlockSpec((None, tc, bd), lambda bb, n: (bb, 0, n)),
                  pl.BlockSpec((CONV_W, bd), vec),
                  pl.BlockSpec((1, bd), vec),
                  pl.BlockSpec((2, None, bd, bd), lambda bb, n: (0, n, 0, 0)),
                  pl.BlockSpec((2, bd), vec),
                  pl.BlockSpec((2, None, bd, bd), lambda bb, n: (0, n, 0, 0)),
                  pl.BlockSpec((2, bd), vec),
                  pl.BlockSpec((2, bd), vec)],
        out_specs=pl.BlockSpec((None, t, bd), lambda bb, n: (bb, 0, n)),
        scratch_shapes=[pltpu.VMEM((t + 2 * LRU_PAD, bd), F32),
                        pltpu.VMEM((tc + 2 * LRU_PAD, bd), F32),
                        pltpu.VMEM((t, bd), F32),
                        pltpu.VMEM((t, bd), F32)],
        compiler_params=_cparams(("parallel", "parallel")),
        name="lru",
    )(p_lat, p_lat, p_ctx, conv_w, conv_b, w_a, b_a, w_x, b_x, lam)


def _gdnpre_kernel(stride, n_heads, x_ref, cw_ref, o_ref, xp_ref, y_ref):
    t = x_ref.shape[0]
    pad = max(2 * stride, SUBLANES)
    _fill_padded(xp_ref, x_ref, t, pad)
    step = min(t, 512)
    pitch = _row_pitch(stride)
    per = step // stride if stride > 1 else 1

    def conv_body(i, carry):
        t0 = pl.multiple_of(i * step, step)
        y = _conv_rows(xp_ref, cw_ref, t0, step, pad, stride)
        if stride == 1:
            y_ref[pl.ds(t0, step), :] = y
        else:
            for j in range(per):
                dst = pl.multiple_of((i * per + j) * pitch, SUBLANES)
                y_ref[pl.ds(dst, stride), :] = y[j * stride:(j + 1) * stride, :]
        return carry

    lax.fori_loop(0, t // step, conv_body, 0)

    if stride > 1:
        w = stride
        n_rows = t // w
        last = t - w
        ylast = (n_rows - 1) * pitch
        col = lax.broadcasted_iota(jnp.int32, (w, 1), 0)
        not_first = col >= 1
        not_last = col <= w - 2
        prev_bot = xp_ref[pl.ds(pad + last - 1, w), :]
        prev_bot2 = xp_ref[pl.ds(pad + last - w - 1, w), :]
        next_top = xp_ref[pl.ds(pad + 1, w), :]
        y_ref[0:w, :] = y_ref[0:w, :] + jnp.where(
            not_first, cw_ref[1:2, :] * prev_bot + cw_ref[0:1, :] * prev_bot2, 0.0)
        y_ref[pitch:pitch + w, :] = y_ref[pitch:pitch + w, :] + jnp.where(
            not_first, cw_ref[0:1, :] * prev_bot, 0.0)
        y_ref[ylast:ylast + w, :] = y_ref[ylast:ylast + w, :] + jnp.where(
            not_last, cw_ref[3:4, :] * next_top, 0.0)

    blk = pl.program_id(1)
    hd = x_ref.shape[1]

    def finish(normed, scale):
        def act(y):
            half = 0.5 * y
            y = half * (1.0 + jnp.tanh(half))
            if normed:
                y = y * (lax.rsqrt(jnp.sum(y * y, axis=-1, keepdims=True) + EPS) * scale)
            return y.astype(BF16)

        if stride == 1:
            def body(i, carry):
                rows = pl.ds(pl.multiple_of(i * step, step), step)
                o_ref[rows, :] = act(y_ref[rows, :])
                return carry

            lax.fori_loop(0, t // step, body, 0)
        else:
            n_rows = t // stride

            def body(w, carry):
                dst = pl.ds(pl.multiple_of(w * n_rows, n_rows), n_rows)
                o_ref[dst, :] = act(y_ref[pl.ds(w, n_rows, stride=pitch), :])
                return carry

            lax.fori_loop(0, stride, body, 0, unroll=16)

    @pl.when(blk < n_heads)
    def _():
        finish(True, hd ** -0.5)

    @pl.when(jnp.logical_and(blk >= n_heads, blk < 2 * n_heads))
    def _():
        finish(True, 1.0)

    @pl.when(blk >= 2 * n_heads)
    def _():
        finish(False, 1.0)


def _gdnpre(p, col0, conv_w, n_heads, stride):
    b, t, _ = p.shape
    hd = LANES
    nblk = 3 * n_heads
    assert col0 % hd == 0 and t % stride == 0
    pad = max(2 * stride, SUBLANES)
    y_rows = (t // stride) * _row_pitch(stride)
    return pl.pallas_call(
        functools.partial(_gdnpre_kernel, stride, n_heads),
        out_shape=jax.ShapeDtypeStruct((b, t, nblk * hd), BF16),
        grid=(b, nblk),
        in_specs=[pl.BlockSpec((None, t, hd), lambda bb, n: (bb, 0, col0 // hd + n)),
                  pl.BlockSpec((CONV_W, hd), lambda bb, n: (0, n))],
        out_specs=pl.BlockSpec((None, t, hd), lambda bb, n: (bb, 0, n)),
        scratch_shapes=[pltpu.VMEM((t + 2 * pad, hd), F32), pltpu.VMEM((y_rows, hd), F32)],
        compiler_params=_cparams(("parallel", "parallel")),
        name="gdnpre",
    )(p, conv_w)


GDN_GROUP_ITEMS = 32


def _split3(x):
    hi = x.astype(BF16)
    r1 = x - hi.astype(F32)
    mid = r1.astype(BF16)
    lo = (r1 - mid.astype(F32)).astype(BF16)
    return hi, mid, lo


def _gdn_kernel(mode, rev, d, gh, n_cast, *refs):
    n_in = {"ctx": 5, "fwd": 6, "bwd": 7}[mode]
    cast_in = refs[n_in:n_in + n_cast]
    cast_out = refs[n_in + n_cast + 1:n_in + 2 * n_cast + 1]
    refs = refs[:n_in] + refs[n_in + n_cast:n_in + n_cast + 1] + refs[n_in + 2 * n_cast + 1:]
    if mode == "ctx":
        q_ref, k_ref, v_ref, ba_ref, prm_ref, sfin_ref, s_scr = refs
    elif mode == "fwd":
        q_ref, k_ref, v_ref, ba_ref, prm_ref, s0_ref, o_ref, s_scr = refs
    else:
        q_ref, k_ref, v_ref, ba_ref, prm_ref, s0_ref, of_ref, o_ref, s_scr = refs
    for src, dst in zip(cast_in, cast_out):
        dst[...] = src[...].astype(BF16)
    r = q_ref.shape[0]
    hd = LANES
    nch = r // CHUNK
    step = pl.program_id(1)
    nsteps = pl.num_programs(1)

    @pl.when(step == 0)
    def _():
        if mode == "ctx":
            s_scr[...] = jnp.zeros(s_scr.shape, F32)
        else:
            s_scr[...] = s0_ref[...]

    ba = ba_ref[...]
    sig = _sigmoid(ba)
    gfull = -jnp.exp(prm_ref[0:1, :]) * _softplus(ba + prm_ref[1:2, :])
    ii = lax.broadcasted_iota(jnp.int32, (r, r), 0)
    jj = lax.broadcasted_iota(jnp.int32, (r, r), 1)
    same = (ii // CHUNK) == (jj // CHUNK)
    tri = jnp.where(jnp.logical_and(same, (jj >= ii) if rev else (jj <= ii)), 1.0, 0.0).astype(BF16)
    g_hi, g_mid, g_lo = _split3(gfull)
    gc = (jnp.dot(tri, g_hi, preferred_element_type=F32)
          + jnp.dot(tri, g_mid, preferred_element_type=F32)
          + jnp.dot(tri, g_lo, preferred_element_type=F32))
    sig_t = sig.T
    gc_t = gc.T

    ci = lax.broadcasted_iota(jnp.int32, (CHUNK, CHUNK), 0)
    cj = lax.broadcasted_iota(jnp.int32, (CHUNK, CHUNK), 1)
    low = (ci <= cj) if rev else (ci >= cj)
    strict = (ci < cj) if rev else (ci > cj)
    eye = jnp.where(ci == cj, 1.0, 0.0).astype(F32)
    last = 0 if rev else CHUNK - 1
    chunk_order = list(range(nch - 1, -1, -1) if rev else range(nch))
    heads_per_group = max(1, min(gh, GDN_GROUP_ITEMS // nch))

    for g0 in range(0, gh, heads_per_group):
        heads = list(range(g0, min(gh, g0 + heads_per_group)))
        items = [(h, c) for h in heads for c in range(nch)]
        st = {}
        for (h, c) in items:
            rows = slice(c * CHUNK, (c + 1) * CHUNK)
            lanes = slice(h * hd, (h + 1) * hd)
            cb = d * gh + h
            cg = 2 * gh + d * gh + h
            bcol = sig[rows, cb:cb + 1]
            gcol = jnp.broadcast_to(gc[rows, cg:cg + 1], (CHUNK, hd))
            grow = gc_t[cg:cg + 1, rows]
            brow = sig_t[cb:cb + 1, rows]
            kc = k_ref[rows, lanes]
            decay = jnp.where(low, jnp.exp(jnp.where(low, gcol[:, :CHUNK] - grow, 0.0)), 0.0)
            x = -jnp.where(strict, _mm_nt(kc, kc) * bcol * decay, 0.0)
            glast = gcol[last:last + 1, :]
            it = dict(rows=rows, lanes=lanes, kc=kc, brow=brow, grow=grow, egl=jnp.exp(glast),
                      egcol=jnp.exp(gcol), etail=jnp.exp(glast - gcol),
                      xb=x.astype(BF16), t=eye + x)
            if mode != "ctx":
                it["intra"] = (_mm_nt(q_ref[rows, lanes], kc) * decay).astype(BF16)
            st[(h, c)] = it
        for _ in range(5):
            for it in st.values():
                it["xb"] = jnp.dot(it["xb"], it["xb"], preferred_element_type=F32).astype(BF16)
            for it in st.values():
                it["t"] = it["t"] + jnp.dot(it["t"].astype(BF16), it["xb"],
                                            preferred_element_type=F32)
        for it in st.values():
            tb = it["t"] * it["brow"]
            it["u"] = _mm(tb, v_ref[it["rows"], it["lanes"]])
            it["w"] = _mm(tb * jnp.exp(it["grow"]), it["kc"]).astype(BF16)
            del it["t"], it["xb"]
        state = {h: s_scr[h] for h in heads}
        for c in chunk_order:
            cur = [st[(h, c)] for h in heads]
            sb = [state[h].astype(BF16) for h in heads]
            ws = [jnp.dot(it["w"], s, preferred_element_type=F32) for it, s in zip(cur, sb)]
            if mode != "ctx":
                qs = [jnp.dot(q_ref[it["rows"], it["lanes"]], s, preferred_element_type=F32)
                      for it, s in zip(cur, sb)]
            vnew = [it["u"] - x for it, x in zip(cur, ws)]
            if mode != "ctx":
                iv = [_mm(it["intra"], v) for it, v in zip(cur, vnew)]
            kv = [_mm_tn(it["kc"], v * it["etail"]) for it, v in zip(cur, vnew)]
            for i, h in enumerate(heads):
                it = cur[i]
                if mode != "ctx":
                    o = it["egcol"] * qs[i] + iv[i]
                    if mode == "bwd":
                        o = o + of_ref[it["rows"], it["lanes"]]
                    o_ref[it["rows"], it["lanes"]] = o
                state[h] = state[h] * it["egl"] + kv[i]
        for h in heads:
            s_scr[h] = state[h]

    if mode == "ctx":
        @pl.when(step == nsteps - 1)
        def _():
            sfin_ref[...] = s_scr[...]


def _gdn_call(mode, rev, d, gh, qkv, ba, prm, r, s0=None, o_f=None, casts=()):
    b, t, _ = qkv.shape
    hd = LANES
    gw = gh * hd
    n_blk = t // r
    assert t % r == 0 and r % CHUNK == 0 and r % LANES == 0
    blk = (lambda w: n_blk - 1 - w) if rev else (lambda w: w)
    in_specs = [pl.BlockSpec((None, r, gw), lambda bb, w: (bb, blk(w), 0)),
                pl.BlockSpec((None, r, gw), lambda bb, w: (bb, blk(w), 1)),
                pl.BlockSpec((None, r, gw), lambda bb, w: (bb, blk(w), 2)),
                pl.BlockSpec((None, r, LANES), lambda bb, w: (bb, blk(w), 0)),
                pl.BlockSpec((SUBLANES, LANES), lambda bb, w: (0, 0))]
    args = [qkv, qkv, qkv, ba, prm]
    state_spec = pl.BlockSpec((None, gh, hd, hd), lambda bb, w: (bb, 0, 0, 0))
    o_spec = pl.BlockSpec((None, r, gw), lambda bb, w: (bb, blk(w), 0))
    if mode != "ctx":
        in_specs.append(state_spec)
        args.append(s0)
    if mode == "bwd":
        in_specs.append(o_spec)
        args.append(o_f)
    if mode == "ctx":
        out_shape = [jax.ShapeDtypeStruct((b, gh, hd, hd), F32)]
        out_specs = [state_spec]
    else:
        out_shape = [jax.ShapeDtypeStruct((b, t, gw), F32)]
        out_specs = [o_spec]
    n_steps = b * n_blk
    for wmat in casts:
        nr, nc = wmat.shape
        rp = _cast_rows(nr, n_steps)
        last = nr // rp - 1
        spec = pl.BlockSpec((rp, nc), lambda bb, w, last=last: (jnp.minimum(bb * n_blk + w, last), 0))
        in_specs.append(spec)
        args.append(wmat)
        out_shape.append(jax.ShapeDtypeStruct((nr, nc), BF16))
        out_specs.append(spec)
    outs = pl.pallas_call(
        functools.partial(_gdn_kernel, mode, rev, d, gh, len(casts)),
        out_shape=tuple(out_shape),
        grid=(b, n_blk),
        in_specs=in_specs,
        out_specs=tuple(out_specs),
        scratch_shapes=[pltpu.VMEM((gh, hd, hd), F32)],
        compiler_params=_cparams(("arbitrary", "arbitrary")),
        name="gdn_%s_%d" % (mode, d),
    )(*args)
    return outs[0] if not casts else outs


def _cast_rows(n_rows, n_steps):
    for rp in range(2 * SUBLANES, n_rows, 2 * SUBLANES):
        if n_rows % rp == 0 and n_rows // rp <= n_steps:
            return rp
    return n_rows


def _to_visit_kernel(width, x_ref, o_ref):
    n_rows = x_ref.shape[0] // width

    def body(w, carry):
        dst = pl.ds(pl.multiple_of(w * n_rows, n_rows), n_rows)
        o_ref[dst, :] = x_ref[pl.ds(w, n_rows, stride=width), :]
        return carry

    lax.fori_loop(0, width, body, 0)


def _to_visit(x, width):
    b, t, c = x.shape
    assert c == LANES
    spec = pl.BlockSpec((None, t, c), lambda bb: (bb, 0, 0))
    return pl.pallas_call(
        functools.partial(_to_visit_kernel, width),
        out_shape=jax.ShapeDtypeStruct((b, t, c), F32),
        grid=(b,),
        in_specs=[spec],
        out_specs=spec,
        compiler_params=_cparams(("parallel",)),
        name="to_visit",
    )(x)


def _gdnpost_kernel(width, o_ref, z_ref, ng_ref, y_ref, scr_ref):
    t = o_ref.shape[0]
    n_rows = t // width
    pitch = _row_pitch(width)

    def scatter(w, carry):
        src = pl.ds(pl.multiple_of(w * n_rows, n_rows), n_rows)
        scr_ref[pl.ds(w, n_rows, stride=pitch), :] = o_ref[src, :]
        return carry

    lax.fori_loop(0, width, scatter, 0, unroll=8)
    per = SUBLANES
    ng = ng_ref[...]

    def body(i, carry):
        parts = [scr_ref[pl.ds(pl.multiple_of((i * per + j) * pitch, SUBLANES), width), :]
                 for j in range(per)]
        o = jnp.concatenate(parts, axis=0)
        rows = pl.ds(pl.multiple_of(i * per * width, per * width), per * width)
        y = o * lax.rsqrt(jnp.mean(o * o, axis=-1, keepdims=True) + EPS) * ng
        y_ref[rows, :] = (y * _silu(z_ref[rows, :])).astype(BF16)
        return carry

    lax.fori_loop(0, n_rows // per, body, 0, unroll=2)


def _gdnpost(o, p, zcol0, ng, gh, width):
    b, t, gw = o.shape
    hd = LANES
    n_rows = t // width
    assert zcol0 % hd == 0 and n_rows % SUBLANES == 0
    return pl.pallas_call(
        functools.partial(_gdnpost_kernel, width),
        out_shape=jax.ShapeDtypeStruct((b, t, gw), BF16),
        grid=(b, gh),
        in_specs=[pl.BlockSpec((None, t, hd), lambda bb, h: (bb, 0, h)),
                  pl.BlockSpec((None, t, hd), lambda bb, h: (bb, 0, zcol0 // hd + h)),
                  pl.BlockSpec((1, hd), lambda bb, h: (0, 0))],
        out_specs=pl.BlockSpec((None, t, hd), lambda bb, h: (bb, 0, h)),
        scratch_shapes=[pltpu.VMEM((n_rows * _row_pitch(width), hd), F32)],
        compiler_params=_cparams(("parallel", "parallel")),
        name="gdnpost",
    )(o, p, ng)


def _outproj_kernel(gl_ref, yg_ref, ng_ref, w_ref, o_ref, a_scr):
    tm, lw = gl_ref.shape

    @pl.when(pl.program_id(1) == 0)
    def _():
        ng = ng_ref[...]

        def body(r, carry):
            rows = pl.ds(pl.multiple_of(r * ROW_CHUNK, ROW_CHUNK), ROW_CHUNK)
            x = gl_ref[rows, :].astype(F32)
            y = x * lax.rsqrt(jnp.mean(x * x, axis=-1, keepdims=True) + EPS) * ng
            a_scr[rows, 0:lw] = y.astype(BF16)
            return carry

        lax.fori_loop(0, tm // ROW_CHUNK, body, 0)
        a_scr[:, lw:] = yg_ref[...]

    o_ref[...] = jnp.dot(a_scr[...], w_ref[...], preferred_element_type=F32).astype(o_ref.dtype)


def _outproj(g_lru, y_gdn, ng, w_out):
    m, lw = g_lru.shape
    gw = y_gdn.shape[1]
    k, n = w_out.shape
    tm = _tile(m, 1024, SUBLANES)
    tn = _tile(n, 512)
    return pl.pallas_call(
        _outproj_kernel,
        out_shape=jax.ShapeDtypeStruct((m, n), BF16),
        grid=(m // tm, n // tn),
        in_specs=[pl.BlockSpec((tm, lw), lambda i, j: (i, 0)),
                  pl.BlockSpec((tm, gw), lambda i, j: (i, 0)),
                  pl.BlockSpec((1, lw), lambda i, j: (0, 0)),
                  pl.BlockSpec((k, tn), lambda i, j: (0, j))],
        out_specs=pl.BlockSpec((tm, tn), lambda i, j: (i, j)),
        scratch_shapes=[pltpu.VMEM((tm, k), BF16)],
        compiler_params=_cparams(("parallel", "arbitrary")),
        name="outproj",
    )(g_lru, y_gdn, ng, w_out)


def _residual(x, y, gpg):
    y = y.astype(F32)
    return x + y * lax.rsqrt(jnp.mean(y * y, axis=-1, keepdims=True) + EPS) * gpg


def _post_kernel(final, *refs):
    if final:
        x_ref, m_ref, gpm_ref, gatem_ref, y_ref, gpf_ref, gatef_ref, o_ref = refs
        gpg_f = gpf_ref[...] * gatef_ref[...]
    else:
        x_ref, m_ref, gpm_ref, gatem_ref, gn_ref, sh_ref, sc_ref, o_ref = refs
        gs, sh = gn_ref[...] * (1.0 + sc_ref[...]), sh_ref[...]
    gpg_m = gpm_ref[...] * gatem_ref[...]

    def body(rows):
        x1 = _residual(x_ref[rows, :], m_ref[rows, :], gpg_m)
        if final:
            o_ref[rows, :] = _residual(x1, y_ref[rows, :], gpg_f)
        else:
            o_ref[rows, :] = _norm_mod_rows(x1, gs, sh).astype(BF16)

    _row_steps(x_ref.shape[0], body)


def _post(x, mix, g_post_mix, gate_m, tail, final):
    b, t, d = x.shape
    tr = _tile(t, 256, SUBLANES)
    row = pl.BlockSpec((None, tr, d), lambda bb, i: (bb, i, 0))
    vec = pl.BlockSpec((1, d), lambda bb, i: (0, 0))
    mod = pl.BlockSpec((None, 1, d), lambda bb, i: (bb, 0, 0))
    return pl.pallas_call(
        functools.partial(_post_kernel, final),
        out_shape=jax.ShapeDtypeStruct((b, t, d), F32 if final else BF16),
        grid=(b, t // tr),
        in_specs=[row, row, vec, mod] + ([row, vec, mod] if final else [vec, mod, mod]),
        out_specs=row,
        compiler_params=_cparams(("parallel", "parallel")),
        name="post" if final else "post_next",
    )(x, mix, g_post_mix, gate_m, *tail)


def _ffn_kernel(overlap, h_ref, wg_ref, wu_ref, wd_ref, o_ref, acc_ref):
    f = pl.program_id(1)
    last = pl.num_programs(1) - 1

    @pl.when(f == 0)
    def _():
        acc_ref[...] = jnp.zeros(acc_ref.shape, F32)

    h = h_ref[...]
    g = jnp.dot(h, wg_ref[...], preferred_element_type=F32)
    u = jnp.dot(h, wu_ref[...], preferred_element_type=F32)
    act = _silu(g) * u
    if overlap:
        col = lax.broadcasted_iota(jnp.int32, act.shape, 1)
        act = jnp.where(col >= jnp.where(f == last, overlap, 0), act, 0.0)
    acc_ref[...] += jnp.dot(act.astype(BF16), wd_ref[...], preferred_element_type=F32)

    @pl.when(f == last)
    def _():
        o_ref[...] = acc_ref[...].astype(o_ref.dtype)


FFN_HIDDEN_TILE = 512


def _ffn(h, wg, wu, wd):
    m, d = h.shape
    dff = wg.shape[1]
    tm = _tile(m, 512, SUBLANES)
    tf = min(FFN_HIDDEN_TILE, dff)
    n_f = pl.cdiv(dff, tf)
    overlap = n_f * tf - dff
    assert overlap % LANES == 0
    assert tf % LANES == 0 and dff % LANES == 0
    start = lambda f: jnp.minimum(f * (tf // LANES), (dff - tf) // LANES) * LANES
    return pl.pallas_call(
        functools.partial(_ffn_kernel, overlap),
        out_shape=jax.ShapeDtypeStruct((m, d), BF16),
        grid=(m // tm, n_f),
        in_specs=[pl.BlockSpec((tm, d), lambda i, f: (i, 0)),
                  pl.BlockSpec((pl.Element(d), pl.Element(tf)), lambda i, f: (0, start(f))),
                  pl.BlockSpec((pl.Element(d), pl.Element(tf)), lambda i, f: (0, start(f))),
                  pl.BlockSpec((pl.Element(tf), pl.Element(d)), lambda i, f: (start(f), 0))],
        out_specs=pl.BlockSpec((tm, d), lambda i, f: (i, 0)),
        scratch_shapes=[pltpu.VMEM((tm, d), F32)],
        compiler_params=_cparams(("parallel", "arbitrary")),
        name="ffn",
    )(h, wg, wu, wd)


def _pad_cols(w, mult):
    n = w.shape[-1]
    padn = (-n) % mult
    return w if padn == 0 else jnp.pad(w, ((0, 0), (0, padn)))


def kernel(x, c, ctx, c_ctx, w_ada, b_ada, g_pre_mix, g_post_mix, g_pre_ffn, g_post_ffn, w_in,
           lru_conv_w, lru_conv_b, lru_w_a, lru_b_a, lru_w_x, lru_b_x, lru_lambda, lru_norm_g,
           gdn_conv_w, gdn_a_log, gdn_dt_bias, gdn_norm_g, w_out, w_ffn_gate, w_ffn_up, w_ffn_down):
    depth = w_ada.shape[0]
    bsz, seq, d = x.shape
    tctx = ctx.shape[1]
    lw = lru_lambda.shape[-1]
    gh = gdn_a_log.shape[-1]
    hd = gdn_norm_g.shape[-1]
    gw = gh * hd
    n_main = 2 * lw + 4 * gw
    rows = seq // GRID_W
    assert hd == LANES and 4 * gh <= LANES and seq == rows * GRID_W
    assert n_main % gw == 0 and (2 * lw + 3 * gw) % gw == 0

    for l in range(depth):
        assert l == depth - 1, "context stream updates (depth > 1) are not implemented"

        crows = jnp.zeros((SUBLANES, d), F32).at[0:bsz].set(c).at[bsz].set(c_ctx)
        mod = _ada(crows, w_ada[l], b_ada[l][None, :])
        mod_lat = mod[0:bsz].reshape(bsz, N_MOD, 1, d)
        shift_m, scale_m, gate_m, shift_f, scale_f, gate_f = (mod_lat[:, i] for i in range(N_MOD))
        mod_ctx = mod[bsz].reshape(N_MOD, 1, 1, d)

        w_all = w_in[l].astype(BF16)
        w_ba = _pad_cols(w_in[l][:, n_main:], LANES).astype(BF16)
        g_pre = g_pre_mix[l][None, :]
        p_lat, ba_lat = _inproj(_prenorm(x, g_pre, shift_m, scale_m), w_all, n_main, w_ba)
        h_ctx = _prenorm(ctx, g_pre, mod_ctx[0], mod_ctx[1]).reshape(1, bsz * tctx, d)
        p_ctx, ba_ctx = _inproj(h_ctx, w_all, n_main, w_ba)
        p_ctx = p_ctx.reshape(bsz, tctx, n_main)
        ba_ctx = ba_ctx.reshape(bsz, tctx, LANES)

        g_lru = _lru(p_lat, p_ctx, lru_conv_w[l], lru_conv_b[l][None, :], lru_w_a[l].astype(BF16),
                     lru_b_a[l], lru_w_x[l].astype(BF16), lru_b_x[l], lru_lambda[l], lw)

        qkv_lat = _gdnpre(p_lat, 2 * lw, gdn_conv_w[l], gh, GRID_W)
        qkv_ctx = _gdnpre(p_ctx, 2 * lw, gdn_conv_w[l], gh, 1)
        prm = jnp.zeros((SUBLANES, LANES), F32)
        prm = prm.at[0, 2 * gh:4 * gh].set(gdn_a_log[l].reshape(-1))
        prm = prm.at[1, 2 * gh:4 * gh].set(gdn_dt_bias[l].reshape(-1))
        ba_vis = _to_visit(ba_lat, GRID_W)
        s_f = _gdn_call("ctx", False, 0, gh, qkv_ctx, ba_ctx, prm, tctx)
        s_b = _gdn_call("ctx", True, 1, gh, qkv_ctx, ba_ctx, prm, tctx)
        o_f, wg_b, wu_b = _gdn_call("fwd", False, 0, gh, qkv_lat, ba_vis, prm, rows, s0=s_f,
                                    casts=(w_ffn_gate[l], w_ffn_up[l]))
        o_fb, wd_b, wo_b = _gdn_call("bwd", True, 1, gh, qkv_lat, ba_vis, prm, rows, s0=s_b, o_f=o_f,
                                     casts=(w_ffn_down[l], w_out[l]))
        y_gdn = _gdnpost(o_fb, p_lat, 2 * lw + 3 * gw, gdn_norm_g[l][None, :], gh, GRID_W)
        y_gdn = y_gdn.reshape(bsz * seq, gw)

        mix = _outproj(g_lru.reshape(bsz * seq, lw), y_gdn, lru_norm_g[l][None, :], wo_b)
        mix = mix.reshape(bsz, seq, d)
        gpm = g_post_mix[l][None, :]
        h_ffn = _post(x, mix, gpm, gate_m, (g_pre_ffn[l][None, :], shift_f, scale_f), final=False)
        y_ffn = _ffn(h_ffn.reshape(bsz * seq, d), wg_b, wu_b, wd_b)
        x = _post(x, mix, gpm, gate_m, (y_ffn.reshape(bsz, seq, d), g_post_ffn[l][None, :], gate_f),
                  final=True)
    return x
```

```python
import functools

import jax
import jax.numpy as jnp
from jax import lax
from jax.experimental import pallas as pl
from jax.experimental.pallas import tpu as pltpu

F32 = jnp.float32
BF16 = jnp.bfloat16

EPS = 1e-6
F32_TINY = 1.1754944e-38
GRID_W = 64
CHUNK = 64
LRU_C = 8.0
CONV_W = 4
N_MOD = 6
LANES = 128
SUBLANES = 8
VMEM_LIMIT = 56 * 1024 * 1024


def _cparams(sem):
    return pltpu.CompilerParams(dimension_semantics=sem, vmem_limit_bytes=VMEM_LIMIT)


def _mm(a, b):
    return jnp.dot(a.astype(BF16), b.astype(BF16), preferred_element_type=F32)


def _mm_nt(a, b):
    return lax.dot_general(a.astype(BF16), b.astype(BF16), (((1,), (1,)), ((), ())),
                           preferred_element_type=F32)


def _mm_tn(a, b):
    return lax.dot_general(a.astype(BF16), b.astype(BF16), (((0,), (0,)), ((), ())),
                           preferred_element_type=F32)


def _sigmoid(x):
    return 1.0 / (1.0 + jnp.exp(-x))


def _silu(x):
    return x * _sigmoid(x)


def _softplus(x):
    return jnp.maximum(x, 0.0) + jnp.log1p(jnp.exp(-jnp.abs(x)))


def _tile(n, target, mult=LANES):
    if n <= target:
        return n
    t = (target // mult) * mult
    while t >= mult:
        if n % t == 0:
            return t
        t -= mult
    return n


def _ada_kernel(c_ref, w_ref, b_ref, o_ref):
    o_ref[...] = _mm(_silu(c_ref[...]), w_ref[...]) + b_ref[...]


def _ada(crows, w, b):
    r, d = crows.shape
    n = w.shape[1]
    tn = _tile(n, 512)
    return pl.pallas_call(
        _ada_kernel,
        out_shape=jax.ShapeDtypeStruct((r, n), F32),
        grid=(n // tn,),
        in_specs=[pl.BlockSpec((r, d), lambda j: (0, 0)),
                  pl.BlockSpec((d, tn), lambda j: (0, j)),
                  pl.BlockSpec((1, tn), lambda j: (0, j))],
        out_specs=pl.BlockSpec((r, tn), lambda j: (0, j)),
        compiler_params=_cparams(("arbitrary",)),
        name="ada",
    )(crows, w, b)


ROW_CHUNK = 64


def _norm_mod_rows(x, gs, sh):
    return x * lax.rsqrt(jnp.mean(x * x, axis=-1, keepdims=True) + EPS) * gs + sh


def _deep(spec):
    return pl.BlockSpec(spec.block_shape, spec.index_map, pipeline_mode=pl.Buffered(3))


NORM_ROWS = 16


def _row_steps(n_rows, body):
    def step(i, carry):
        body(pl.ds(pl.multiple_of(i * NORM_ROWS, NORM_ROWS), NORM_ROWS))
        return carry

    lax.fori_loop(0, n_rows // NORM_ROWS, step, 0, unroll=4)


def _prenorm_kernel(x_ref, g_ref, sh_ref, sc_ref, o_ref):
    gs, sh = g_ref[...] * (1.0 + sc_ref[...]), sh_ref[...]

    def body(rows):
        o_ref[rows, :] = _norm_mod_rows(x_ref[rows, :], gs, sh).astype(BF16)

    _row_steps(x_ref.shape[0], body)


def _prenorm(x, g, sh, sc):
    b, t, d = x.shape
    tr = _tile(t, 256, SUBLANES)
    per_batch = sh.shape[0] > 1
    mod_map = (lambda bb, i: (bb, 0, 0)) if per_batch else (lambda bb, i: (0, 0, 0))
    row = pl.BlockSpec((None, tr, d), lambda bb, i: (bb, i, 0))
    return pl.pallas_call(
        _prenorm_kernel,
        out_shape=jax.ShapeDtypeStruct((b, t, d), BF16),
        grid=(b, t // tr),
        in_specs=[row, pl.BlockSpec((1, d), lambda bb, i: (0, 0)),
                  pl.BlockSpec((None, 1, d), mod_map), pl.BlockSpec((None, 1, d), mod_map)],
        out_specs=row,
        compiler_params=_cparams(("parallel", "parallel")),
        name="prenorm",
    )(x, g, sh, sc)


def _inproj_kernel(h_ref, w_ref, wba_ref, o_ref, oba_ref):
    @pl.when(pl.program_id(2) == 0)
    def _():
        oba_ref[...] = jnp.dot(h_ref[...], wba_ref[...], preferred_element_type=F32)

    o_ref[...] = jnp.dot(h_ref[...], w_ref[...], preferred_element_type=F32)


def _inproj(h, w, n, w_ba):
    b, t, d = h.shape
    nba = w_ba.shape[1]
    tm = _tile(t, 1024, SUBLANES)
    tn = _tile(n, 768 if tm > 512 else 1536)
    return pl.pallas_call(
        _inproj_kernel,
        out_shape=(jax.ShapeDtypeStruct((b, t, n), F32), jax.ShapeDtypeStruct((b, t, nba), F32)),
        grid=(b, t // tm, n // tn),
        in_specs=[pl.BlockSpec((None, tm, d), lambda bb, i, j: (bb, i, 0)),
                  pl.BlockSpec((d, tn), lambda bb, i, j: (0, j)),
                  pl.BlockSpec((d, nba), lambda bb, i, j: (0, 0))],
        out_specs=(pl.BlockSpec((None, tm, tn), lambda bb, i, j: (bb, i, j)),
                   pl.BlockSpec((None, tm, nba), lambda bb, i, j: (bb, i, 0))),
        compiler_params=_cparams(("parallel", "parallel", "arbitrary")),
        name="inproj",
    )(h, w, w_ba)


def _fill_padded(xp_ref, x_ref, n, pad):
    zeros = jnp.zeros((pad, xp_ref.shape[1]), F32)
    xp_ref[0:pad, :] = zeros
    xp_ref[pad + n:pad + n + pad, :] = zeros
    step = min(n, 512)

    def body(i, carry):
        rows = pl.ds(pl.multiple_of(i * step, step), step)
        xp_ref[pl.ds(pl.multiple_of(pad + i * step, SUBLANES), step), :] = x_ref[rows, :]
        return carry

    lax.fori_loop(0, n // step, body, 0)


def _row_pitch(stride):
    return stride if stride == 1 else stride + SUBLANES


def _conv_rows(xp_ref, cw_ref, t0, rows, pad, stride):
    acc = None
    for j in range(CONV_W):
        off = pad + (j - 2) * stride
        term = cw_ref[j:j + 1, :] * xp_ref[pl.ds(t0 + off, rows), :]
        acc = term if acc is None else acc + term
    return acc


LRU_ROWS = 2048
LRU_PAD = 8


def _lru_coeffs(xc, d, wa_ref, ba_ref, wx_ref, bx_ref, lam_ref):
    xb = xc.astype(BF16)
    r = _sigmoid(jnp.dot(xb, wa_ref[d], preferred_element_type=F32) + ba_ref[d:d + 1, :])
    i = _sigmoid(jnp.dot(xb, wx_ref[d], preferred_element_type=F32) + bx_ref[d:d + 1, :])
    log_a = r * ((-LRU_C) * _softplus(-lam_ref[d:d + 1, :]))
    a = jnp.exp(log_a)
    th = jnp.tanh(log_a)
    y = -2.0 * th / (1.0 - th)
    return a, (y * lax.rsqrt(jnp.maximum(y, F32_TINY))) * (i * xc)


def _lru_scan_chunk(a, b, h, rev):
    rows, c = a.shape
    nt = rows // SUBLANES
    a3 = a.reshape(nt, SUBLANES, c)
    b3 = b.reshape(nt, SUBLANES, c)
    ridx = lax.broadcasted_iota(jnp.int32, (nt, SUBLANES, c), 1)
    for s in (1, 2, 4):
        shift = (SUBLANES - s) if rev else s
        a_s = pltpu.roll(a3, shift, 1)
        b_s = pltpu.roll(b3, shift, 1)
        m = (ridx < SUBLANES - s) if rev else (ridx >= s)
        b3 = jnp.where(m, a3 * b_s + b3, b3)
        a3 = jnp.where(m, a3 * a_s, a3)
    out = [None] * nt
    order = range(nt - 1, -1, -1) if rev else range(nt)
    for t in order:
        ht = b3[t] + a3[t] * h
        out[t] = ht
        h = ht[0:1, :] if rev else ht[SUBLANES - 1:SUBLANES, :]
    return jnp.concatenate(out, axis=0), h


def _lru_kernel(ul_ref, gl_ref, uc_ref, cw_ref, cb_ref, wa_ref, ba_ref, wx_ref, bx_ref, lam_ref,
                o_ref, xpl_ref, xpc_ref, xc_ref, hf_ref):
    t_lat = ul_ref.shape[0]
    t_ctx = uc_ref.shape[0]
    c = ul_ref.shape[1]
    rl = min(LRU_ROWS, t_lat)
    rc = min(LRU_ROWS, t_ctx)
    _fill_padded(xpl_ref, ul_ref, t_lat, LRU_PAD)
    _fill_padded(xpc_ref, uc_ref, t_ctx, LRU_PAD)
    cb = cb_ref[...]

    for d in range(2):
        rev = d == 1
        prm = (wa_ref, ba_ref, wx_ref, bx_ref, lam_ref)

        def ctx_body(i, h, rev=rev, d=d, prm=prm):
            blk = (t_ctx // rc - 1 - i) if rev else i
            t0 = pl.multiple_of(blk * rc, rc)
            xc = _conv_rows(xpc_ref, cw_ref, t0, rc, LRU_PAD, 1) + cb
            a, b = _lru_coeffs(xc, d, *prm)
            _, h = _lru_scan_chunk(a, b, h, rev)
            return h

        h = lax.fori_loop(0, t_ctx // rc, ctx_body, jnp.zeros((1, c), F32))

        def lat_body(i, h, rev=rev, d=d, prm=prm):
            blk = (t_lat // rl - 1 - i) if rev else i
            t0 = pl.multiple_of(blk * rl, rl)
            rows = pl.ds(t0, rl)
            if rev:
                xc = xc_ref[rows, :]
            else:
                xc = _conv_rows(xpl_ref, cw_ref, t0, rl, LRU_PAD, 1) + cb
                xc_ref[rows, :] = xc
            a, b = _lru_coeffs(xc, d, *prm)
            hs, h = _lru_scan_chunk(a, b, h, rev)
            if rev:
                y = (hf_ref[rows, :] + hs) * jax.nn.gelu(gl_ref[rows, :], approximate=True)
                o_ref[rows, :] = y.astype(BF16)
            else:
                hf_ref[rows, :] = hs
            return h

        lax.fori_loop(0, t_lat // rl, lat_body, h)


def _lru(p_lat, p_ctx, conv_w, conv_b, w_a, b_a, w_x, b_x, lam, lw):
    b, t, _ = p_lat.shape
    tc = p_ctx.shape[1]
    nb, bd = w_a.shape[1], w_a.shape[2]
    assert bd == LANES and nb * bd == lw
    vec = lambda bb, n: (0, n)
    return pl.pallas_call(
        _lru_kernel,
        out_shape=jax.ShapeDtypeStruct((b, t, lw), BF16),
        grid=(b, nb),
        in_specs=[pl.BlockSpec((None, t, bd), lambda bb, n: (bb, 0, n)),
                  pl.BlockSpec((None, t, bd), lambda bb, n: (bb, 0, nb + n)),
                  pl.BlockSpec((None, tc, bd), lambda bb, n: (bb, 0, n)),
                  pl.BlockSpec((CONV_W, bd), vec),
                  pl.BlockSpec((1, bd), vec),
                  pl.BlockSpec((2, None, bd, bd), lambda bb, n: (0, n, 0, 0)),
                  pl.BlockSpec((2, bd), vec),
                  pl.BlockSpec((2, None, bd, bd), lambda bb, n: (0, n, 0, 0)),
                  pl.BlockSpec((2, bd), vec),
                  pl.BlockSpec((2, bd), vec)],
        out_specs=pl.BlockSpec((None, t, bd), lambda bb, n: (bb, 0, n)),
        scratch_shapes=[pltpu.VMEM((t + 2 * LRU_PAD, bd), F32),
                        pltpu.VMEM((tc + 2 * LRU_PAD, bd), F32),
                        pltpu.VMEM((t, bd), F32),
                        pltpu.VMEM((t, bd), F32)],
        compiler_params=_cparams(("parallel", "parallel")),
        name="lru",
    )(p_lat, p_lat, p_ctx, conv_w, conv_b, w_a, b_a, w_x, b_x, lam)


def _gdnpre_kernel(stride, n_heads, x_ref, cw_ref, o_ref, xp_ref, y_ref):
    t = x_ref.shape[0]
    pad = max(2 * stride, SUBLANES)
    _fill_padded(xp_ref, x_ref, t, pad)
    step = min(t, 512)
    pitch = _row_pitch(stride)
    per = step // stride if stride > 1 else 1

    def conv_body(i, carry):
        t0 = pl.multiple_of(i * step, step)
        y = _conv_rows(xp_ref, cw_ref, t0, step, pad, stride)
        if stride == 1:
            y_ref[pl.ds(t0, step), :] = y
        else:
            for j in range(per):
                dst = pl.multiple_of((i * per + j) * pitch, SUBLANES)
                y_ref[pl.ds(dst, stride), :] = y[j * stride:(j + 1) * stride, :]
        return carry

    lax.fori_loop(0, t // step, conv_body, 0)

    if stride > 1:
        w = stride
        n_rows = t // w
        last = t - w
        ylast = (n_rows - 1) * pitch
        col = lax.broadcasted_iota(jnp.int32, (w, 1), 0)
        not_first = col >= 1
        not_last = col <= w - 2
        prev_bot = xp_ref[pl.ds(pad + last - 1, w), :]
        prev_bot2 = xp_ref[pl.ds(pad + last - w - 1, w), :]
        next_top = xp_ref[pl.ds(pad + 1, w), :]
        y_ref[0:w, :] = y_ref[0:w, :] + jnp.where(
            not_first, cw_ref[1:2, :] * prev_bot + cw_ref[0:1, :] * prev_bot2, 0.0)
        y_ref[pitch:pitch + w, :] = y_ref[pitch:pitch + w, :] + jnp.where(
            not_first, cw_ref[0:1, :] * prev_bot, 0.0)
        y_ref[ylast:ylast + w, :] = y_ref[ylast:ylast + w, :] + jnp.where(
            not_last, cw_ref[3:4, :] * next_top, 0.0)

    blk = pl.program_id(1)
    hd = x_ref.shape[1]

    def finish(normed, scale):
        def act(y):
            half = 0.5 * y
            y = half * (1.0 + jnp.tanh(half))
            if normed:
                y = y * (lax.rsqrt(jnp.sum(y * y, axis=-1, keepdims=True) + EPS) * scale)
            return y.astype(BF16)

        if stride == 1:
            def body(i, carry):
                rows = pl.ds(pl.multiple_of(i * step, step), step)
                o_ref[rows, :] = act(y_ref[rows, :])
                return carry

            lax.fori_loop(0, t // step, body, 0)
        else:
            n_rows = t // stride

            def body(w, carry):
                dst = pl.ds(pl.multiple_of(w * n_rows, n_rows), n_rows)
                o_ref[dst, :] = act(y_ref[pl.ds(w, n_rows, stride=pitch), :])
                return carry

            lax.fori_loop(0, stride, body, 0, unroll=16)

    @pl.when(blk < n_heads)
    def _():
        finish(True, hd ** -0.5)

    @pl.when(jnp.logical_and(blk >= n_heads, blk < 2 * n_heads))
    def _():
        finish(True, 1.0)

    @pl.when(blk >= 2 * n_heads)
    def _():
        finish(False, 1.0)


def _gdnpre(p, col0, conv_w, n_heads, stride):
    b, t, _ = p.shape
    hd = LANES
    nblk = 3 * n_heads
    assert col0 % hd == 0 and t % stride == 0
    pad = max(2 * stride, SUBLANES)
    y_rows = (t // stride) * _row_pitch(stride)
    return pl.pallas_call(
        functools.partial(_gdnpre_kernel, stride, n_heads),
        out_shape=jax.ShapeDtypeStruct((b, t, nblk * hd), BF16),
        grid=(b, nblk),
        in_specs=[pl.BlockSpec((None, t, hd), lambda bb, n: (bb, 0, col0 // hd + n)),
                  pl.BlockSpec((CONV_W, hd), lambda bb, n: (0, n))],
        out_specs=pl.BlockSpec((None, t, hd), lambda bb, n: (bb, 0, n)),
        scratch_shapes=[pltpu.VMEM((t + 2 * pad, hd), F32), pltpu.VMEM((y_rows, hd), F32)],
        compiler_params=_cparams(("parallel", "parallel")),
        name="gdnpre",
    )(p, conv_w)


GDN_GROUP_ITEMS = 32


def _split3(x):
    hi = x.astype(BF16)
    r1 = x - hi.astype(F32)
    mid = r1.astype(BF16)
    lo = (r1 - mid.astype(F32)).astype(BF16)
    return hi, mid, lo


def _gdn_kernel(mode, rev, d, gh, n_cast, *refs):
    n_in = {"ctx": 5, "fwd": 6, "bwd": 7}[mode]
    cast_in = refs[n_in:n_in + n_cast]
    cast_out = refs[n_in + n_cast + 1:n_in + 2 * n_cast + 1]
    refs = refs[:n_in] + refs[n_in + n_cast:n_in + n_cast + 1] + refs[n_in + 2 * n_cast + 1:]
    if mode == "ctx":
        q_ref, k_ref, v_ref, ba_ref, prm_ref, sfin_ref, s_scr = refs
    elif mode == "fwd":
        q_ref, k_ref, v_ref, ba_ref, prm_ref, s0_ref, o_ref, s_scr = refs
    else:
        q_ref, k_ref, v_ref, ba_ref, prm_ref, s0_ref, of_ref, o_ref, s_scr = refs
    for src, dst in zip(cast_in, cast_out):
        dst[...] = src[...].astype(BF16)
    r = q_ref.shape[0]
    hd = LANES
    nch = r // CHUNK
    step = pl.program_id(1)
    nsteps = pl.num_programs(1)

    @pl.when(step == 0)
    def _():
        if mode == "ctx":
            s_scr[...] = jnp.zeros(s_scr.shape, F32)
        else:
            s_scr[...] = s0_ref[...]

    ba = ba_ref[...]
    sig = _sigmoid(ba)
    gfull = -jnp.exp(prm_ref[0:1, :]) * _softplus(ba + prm_ref[1:2, :])
    ii = lax.broadcasted_iota(jnp.int32, (r, r), 0)
    jj = lax.broadcasted_iota(jnp.int32, (r, r), 1)
    same = (ii // CHUNK) == (jj // CHUNK)
    tri = jnp.where(jnp.logical_and(same, (jj >= ii) if rev else (jj <= ii)), 1.0, 0.0).astype(BF16)
    g_hi, g_mid, g_lo = _split3(gfull)
    gc = (jnp.dot(tri, g_hi, preferred_element_type=F32)
          + jnp.dot(tri, g_mid, preferred_element_type=F32)
          + jnp.dot(tri, g_lo, preferred_element_type=F32))
    sig_t = sig.T
    gc_t = gc.T

    ci = lax.broadcasted_iota(jnp.int32, (CHUNK, CHUNK), 0)
    cj = lax.broadcasted_iota(jnp.int32, (CHUNK, CHUNK), 1)
    low = (ci <= cj) if rev else (ci >= cj)
    strict = (ci < cj) if rev else (ci > cj)
    eye = jnp.where(ci == cj, 1.0, 0.0).astype(F32)
    last = 0 if rev else CHUNK - 1
    chunk_order = list(range(nch - 1, -1, -1) if rev else range(nch))
    heads_per_group = max(1, min(gh, GDN_GROUP_ITEMS // nch))

    for g0 in range(0, gh, heads_per_group):
        heads = list(range(g0, min(gh, g0 + heads_per_group)))
        items = [(h, c) for h in heads for c in range(nch)]
        st = {}
        for (h, c) in items:
            rows = slice(c * CHUNK, (c + 1) * CHUNK)
            lanes = slice(h * hd, (h + 1) * hd)
            cb = d * gh + h
            cg = 2 * gh + d * gh + h
            bcol = sig[rows, cb:cb + 1]
            gcol = jnp.broadcast_to(gc[rows, cg:cg + 1], (CHUNK, hd))
            grow = gc_t[cg:cg + 1, rows]
            brow = sig_t[cb:cb + 1, rows]
            kc = k_ref[rows, lanes]
            decay = jnp.where(low, jnp.exp(jnp.where(low, gcol[:, :CHUNK] - grow, 0.0)), 0.0)
            x = -jnp.where(strict, _mm_nt(kc, kc) * bcol * decay, 0.0)
            glast = gcol[last:last + 1, :]
            it = dict(rows=rows, lanes=lanes, kc=kc, brow=brow, grow=grow, egl=jnp.exp(glast),
                      egcol=jnp.exp(gcol), etail=jnp.exp(glast - gcol),
                      xb=x.astype(BF16), t=eye + x)
            if mode != "ctx":
                it["intra"] = (_mm_nt(q_ref[rows, lanes], kc) * decay).astype(BF16)
            st[(h, c)] = it
        for _ in range(5):
            for it in st.values():
                it["xb"] = jnp.dot(it["xb"], it["xb"], preferred_element_type=F32).astype(BF16)
            for it in st.values():
                it["t"] = it["t"] + jnp.dot(it["t"].astype(BF16), it["xb"],
                                            preferred_element_type=F32)
        for it in st.values():
            tb = it["t"] * it["brow"]
            it["u"] = _mm(tb, v_ref[it["rows"], it["lanes"]])
            it["w"] = _mm(tb * jnp.exp(it["grow"]), it["kc"]).astype(BF16)
            del it["t"], it["xb"]
        state = {h: s_scr[h] for h in heads}
        for c in chunk_order:
            cur = [st[(h, c)] for h in heads]
            sb = [state[h].astype(BF16) for h in heads]
            ws = [jnp.dot(it["w"], s, preferred_element_type=F32) for it, s in zip(cur, sb)]
            if mode != "ctx":
                qs = [jnp.dot(q_ref[it["rows"], it["lanes"]], s, preferred_element_type=F32)
                      for it, s in zip(cur, sb)]
            vnew = [it["u"] - x for it, x in zip(cur, ws)]
            if mode != "ctx":
                iv = [_mm(it["intra"], v) for it, v in zip(cur, vnew)]
            kv = [_mm_tn(it["kc"], v * it["etail"]) for it, v in zip(cur, vnew)]
            for i, h in enumerate(heads):
                it = cur[i]
                if mode != "ctx":
                    o = it["egcol"] * qs[i] + iv[i]
                    if mode == "bwd":
                        o = o + of_ref[it["rows"], it["lanes"]]
                    o_ref[it["rows"], it["lanes"]] = o
                state[h] = state[h] * it["egl"] + kv[i]
        for h in heads:
            s_scr[h] = state[h]

    if mode == "ctx":
        @pl.when(step == nsteps - 1)
        def _():
            sfin_ref[...] = s_scr[...]


def _gdn_call(mode, rev, d, gh, qkv, ba, prm, r, s0=None, o_f=None, casts=()):
    b, t, _ = qkv.shape
    hd = LANES
    gw = gh * hd
    n_blk = t // r
    assert t % r == 0 and r % CHUNK == 0 and r % LANES == 0
    blk = (lambda w: n_blk - 1 - w) if rev else (lambda w: w)
    in_specs = [pl.BlockSpec((None, r, gw), lambda bb, w: (bb, blk(w), 0)),
                pl.BlockSpec((None, r, gw), lambda bb, w: (bb, blk(w), 1)),
                pl.BlockSpec((None, r, gw), lambda bb, w: (bb, blk(w), 2)),
                pl.BlockSpec((None, r, LANES), lambda bb, w: (bb, blk(w), 0)),
                pl.BlockSpec((SUBLANES, LANES), lambda bb, w: (0, 0))]
    args = [qkv, qkv, qkv, ba, prm]
    state_spec = pl.BlockSpec((None, gh, hd, hd), lambda bb, w: (bb, 0, 0, 0))
    o_spec = pl.BlockSpec((None, r, gw), lambda bb, w: (bb, blk(w), 0))
    if mode != "ctx":
        in_specs.append(state_spec)
        args.append(s0)
    if mode == "bwd":
        in_specs.append(o_spec)
        args.append(o_f)
    if mode == "ctx":
        out_shape = [jax.ShapeDtypeStruct((b, gh, hd, hd), F32)]
        out_specs = [state_spec]
    else:
        out_shape = [jax.ShapeDtypeStruct((b, t, gw), F32)]
        out_specs = [o_spec]
    n_steps = b * n_blk
    for wmat in casts:
        nr, nc = wmat.shape
        rp = _cast_rows(nr, n_steps)
        last = nr // rp - 1
        spec = pl.BlockSpec((rp, nc), lambda bb, w, last=last: (jnp.minimum(bb * n_blk + w, last), 0))
        in_specs.append(spec)
        args.append(wmat)
        out_shape.append(jax.ShapeDtypeStruct((nr, nc), BF16))
        out_specs.append(spec)
    outs = pl.pallas_call(
        functools.partial(_gdn_kernel, mode, rev, d, gh, len(casts)),
        out_shape=tuple(out_shape),
        grid=(b, n_blk),
        in_specs=in_specs,
        out_specs=tuple(out_specs),
        scratch_shapes=[pltpu.VMEM((gh, hd, hd), F32)],
        compiler_params=_cparams(("arbitrary", "arbitrary")),
        name="gdn_%s_%d" % (mode, d),
    )(*args)
    return outs[0] if not casts else outs


def _cast_rows(n_rows, n_steps):
    for rp in range(2 * SUBLANES, n_rows, 2 * SUBLANES):
        if n_rows % rp == 0 and n_rows // rp <= n_steps:
            return rp
    return n_rows


def _to_visit_kernel(width, x_ref, o_ref):
    n_rows = x_ref.shape[0] // width

    def body(w, carry):
        dst = pl.ds(pl.multiple_of(w * n_rows, n_rows), n_rows)
        o_ref[dst, :] = x_ref[pl.ds(w, n_rows, stride=width), :]
        return carry

    lax.fori_loop(0, width, body, 0)


def _to_visit(x, width):
    b, t, c = x.shape
    assert c == LANES
    spec = pl.BlockSpec((None, t, c), lambda bb: (bb, 0, 0))
    return pl.pallas_call(
        functools.partial(_to_visit_kernel, width),
        out_shape=jax.ShapeDtypeStruct((b, t, c), F32),
        grid=(b,),
        in_specs=[spec],
        out_specs=spec,
        compiler_params=_cparams(("parallel",)),
        name="to_visit",
    )(x)


def _gdnpost_kernel(width, o_ref, z_ref, ng_ref, y_ref, scr_ref):
    t = o_ref.shape[0]
    n_rows = t // width
    pitch = _row_pitch(width)

    def scatter(w, carry):
        src = pl.ds(pl.multiple_of(w * n_rows, n_rows), n_rows)
        scr_ref[pl.ds(w, n_rows, stride=pitch), :] = o_ref[src, :]
        return carry

    lax.fori_loop(0, width, scatter, 0, unroll=8)
    per = SUBLANES
    ng = ng_ref[...]

    def body(i, carry):
        parts = [scr_ref[pl.ds(pl.multiple_of((i * per + j) * pitch, SUBLANES), width), :]
                 for j in range(per)]
        o = jnp.concatenate(parts, axis=0)
        rows = pl.ds(pl.multiple_of(i * per * width, per * width), per * width)
        y = o * lax.rsqrt(jnp.mean(o * o, axis=-1, keepdims=True) + EPS) * ng
        y_ref[rows, :] = (y * _silu(z_ref[rows, :])).astype(BF16)
        return carry

    lax.fori_loop(0, n_rows // per, body, 0, unroll=2)


def _gdnpost(o, p, zcol0, ng, gh, width):
    b, t, gw = o.shape
    hd = LANES
    n_rows = t // width
    assert zcol0 % hd == 0 and n_rows % SUBLANES == 0
    return pl.pallas_call(
        functools.partial(_gdnpost_kernel, width),
        out_shape=jax.ShapeDtypeStruct((b, t, gw), BF16),
        grid=(b, gh),
        in_specs=[pl.BlockSpec((None, t, hd), lambda bb, h: (bb, 0, h)),
                  pl.BlockSpec((None, t, hd), lambda bb, h: (bb, 0, zcol0 // hd + h)),
                  pl.BlockSpec((1, hd), lambda bb, h: (0, 0))],
        out_specs=pl.BlockSpec((None, t, hd), lambda bb, h: (bb, 0, h)),
        scratch_shapes=[pltpu.VMEM((n_rows * _row_pitch(width), hd), F32)],
        compiler_params=_cparams(("parallel", "parallel")),
        name="gdnpost",
    )(o, p, ng)


def _outproj_kernel(gl_ref, yg_ref, ng_ref, w_ref, o_ref, a_scr):
    tm, lw = gl_ref.shape

    @pl.when(pl.program_id(1) == 0)
    def _():
        ng = ng_ref[...]

        def body(r, carry):
            rows = pl.ds(pl.multiple_of(r * ROW_CHUNK, ROW_CHUNK), ROW_CHUNK)
            x = gl_ref[rows, :].astype(F32)
            y = x * lax.rsqrt(jnp.mean(x * x, axis=-1, keepdims=True) + EPS) * ng
            a_scr[rows, 0:lw] = y.astype(BF16)
            return carry

        lax.fori_loop(0, tm // ROW_CHUNK, body, 0)
        a_scr[:, lw:] = yg_ref[...]

    o_ref[...] = jnp.dot(a_scr[...], w_ref[...], preferred_element_type=F32).astype(o_ref.dtype)


def _outproj(g_lru, y_gdn, ng, w_out):
    m, lw = g_lru.shape
    gw = y_gdn.shape[1]
    k, n = w_out.shape
    tm = _tile(m, 1024, SUBLANES)
    tn = _tile(n, 512)
    return pl.pallas_call(
        _outproj_kernel,
        out_shape=jax.ShapeDtypeStruct((m, n), BF16),
        grid=(m // tm, n // tn),
        in_specs=[pl.BlockSpec((tm, lw), lambda i, j: (i, 0)),
                  pl.BlockSpec((tm, gw), lambda i, j: (i, 0)),
                  pl.BlockSpec((1, lw), lambda i, j: (0, 0)),
                  pl.BlockSpec((k, tn), lambda i, j: (0, j))],
        out_specs=pl.BlockSpec((tm, tn), lambda i, j: (i, j)),
        scratch_shapes=[pltpu.VMEM((tm, k), BF16)],
        compiler_params=_cparams(("parallel", "arbitrary")),
        name="outproj",
    )(g_lru, y_gdn, ng, w_out)


def _residual(x, y, gpg):
    y = y.astype(F32)
    return x + y * lax.rsqrt(jnp.mean(y * y, axis=-1, keepdims=True) + EPS) * gpg


def _post_kernel(final, *refs):
    if final:
        x_ref, m_ref, gpm_ref, gatem_ref, y_ref, gpf_ref, gatef_ref, o_ref = refs
        gpg_f = gpf_ref[...] * gatef_ref[...]
    else:
        x_ref, m_ref, gpm_ref, gatem_ref, gn_ref, sh_ref, sc_ref, o_ref = refs
        gs, sh = gn_ref[...] * (1.0 + sc_ref[...]), sh_ref[...]
    gpg_m = gpm_ref[...] * gatem_ref[...]

    def body(rows):
        x1 = _residual(x_ref[rows, :], m_ref[rows, :], gpg_m)
        if final:
            o_ref[rows, :] = _residual(x1, y_ref[rows, :], gpg_f)
        else:
            o_ref[rows, :] = _norm_mod_rows(x1, gs, sh).astype(BF16)

    _row_steps(x_ref.shape[0], body)


def _post(x, mix, g_post_mix, gate_m, tail, final):
    b, t, d = x.shape
    tr = _tile(t, 256, SUBLANES)
    row = pl.BlockSpec((None, tr, d), lambda bb, i: (bb, i, 0))
    vec = pl.BlockSpec((1, d), lambda bb, i: (0, 0))
    mod = pl.BlockSpec((None, 1, d), lambda bb, i: (bb, 0, 0))
    in_specs = ([_deep(row), _deep(row), vec, mod]
                + ([_deep(row), vec, mod] if final else [vec, mod, mod]))
    n_in = len(in_specs)

    def piped(*refs):
        pltpu.emit_pipeline(functools.partial(_post_kernel, final), grid=(b, t // tr),
                            in_specs=in_specs, out_specs=[row])(*refs[:n_in], refs[n_in])

    anyspec = pl.BlockSpec(memory_space=pl.ANY)
    return pl.pallas_call(
        piped,
        out_shape=jax.ShapeDtypeStruct((b, t, d), F32 if final else BF16),
        in_specs=[anyspec] * n_in,
        out_specs=anyspec,
        compiler_params=pltpu.CompilerParams(vmem_limit_bytes=VMEM_LIMIT),
        name="post" if final else "post_next",
    )(x, mix, g_post_mix, gate_m, *tail)


def _ffn_kernel(overlap, h_ref, wg_ref, wu_ref, wd_ref, o_ref, acc_ref):
    f = pl.program_id(1)
    last = pl.num_programs(1) - 1

    @pl.when(f == 0)
    def _():
        acc_ref[...] = jnp.zeros(acc_ref.shape, F32)

    h = h_ref[...]
    g = jnp.dot(h, wg_ref[...], preferred_element_type=F32)
    u = jnp.dot(h, wu_ref[...], preferred_element_type=F32)
    act = _silu(g) * u
    if overlap:
        col = lax.broadcasted_iota(jnp.int32, act.shape, 1)
        act = jnp.where(col >= jnp.where(f == last, overlap, 0), act, 0.0)
    acc_ref[...] += jnp.dot(act.astype(BF16), wd_ref[...], preferred_element_type=F32)

    @pl.when(f == last)
    def _():
        o_ref[...] = acc_ref[...].astype(o_ref.dtype)


FFN_HIDDEN_TILE = 512


def _ffn(h, wg, wu, wd):
    m, d = h.shape
    dff = wg.shape[1]
    tm = _tile(m, 512, SUBLANES)
    tf = min(FFN_HIDDEN_TILE, dff)
    n_f = pl.cdiv(dff, tf)
    overlap = n_f * tf - dff
    assert overlap % LANES == 0
    assert tf % LANES == 0 and dff % LANES == 0
    start = lambda f: jnp.minimum(f * (tf // LANES), (dff - tf) // LANES) * LANES
    return pl.pallas_call(
        functools.partial(_ffn_kernel, overlap),
        out_shape=jax.ShapeDtypeStruct((m, d), BF16),
        grid=(m // tm, n_f),
        in_specs=[pl.BlockSpec((tm, d), lambda i, f: (i, 0)),
                  pl.BlockSpec((pl.Element(d), pl.Element(tf)), lambda i, f: (0, start(f))),
                  pl.BlockSpec((pl.Element(d), pl.Element(tf)), lambda i, f: (0, start(f))),
                  pl.BlockSpec((pl.Element(tf), pl.Element(d)), lambda i, f: (start(f), 0))],
        out_specs=pl.BlockSpec((tm, d), lambda i, f: (i, 0)),
        scratch_shapes=[pltpu.VMEM((tm, d), F32)],
        compiler_params=_cparams(("parallel", "arbitrary")),
        name="ffn",
    )(h, wg, wu, wd)


def _pad_cols(w, mult):
    n = w.shape[-1]
    padn = (-n) % mult
    return w if padn == 0 else jnp.pad(w, ((0, 0), (0, padn)))


def kernel(x, c, ctx, c_ctx, w_ada, b_ada, g_pre_mix, g_post_mix, g_pre_ffn, g_post_ffn, w_in,
           lru_conv_w, lru_conv_b, lru_w_a, lru_b_a, lru_w_x, lru_b_x, lru_lambda, lru_norm_g,
           gdn_conv_w, gdn_a_log, gdn_dt_bias, gdn_norm_g, w_out, w_ffn_gate, w_ffn_up, w_ffn_down):
    depth = w_ada.shape[0]
    bsz, seq, d = x.shape
    tctx = ctx.shape[1]
    lw = lru_lambda.shape[-1]
    gh = gdn_a_log.shape[-1]
    hd = gdn_norm_g.shape[-1]
    gw = gh * hd
    n_main = 2 * lw + 4 * gw
    rows = seq // GRID_W
    assert hd == LANES and 4 * gh <= LANES and seq == rows * GRID_W
    assert n_main % gw == 0 and (2 * lw + 3 * gw) % gw == 0

    for l in range(depth):
        assert l == depth - 1, "context stream updates (depth > 1) are not implemented"

        crows = jnp.zeros((SUBLANES, d), F32).at[0:bsz].set(c).at[bsz].set(c_ctx)
        mod = _ada(crows, w_ada[l], b_ada[l][None, :])
        mod_lat = mod[0:bsz].reshape(bsz, N_MOD, 1, d)
        shift_m, scale_m, gate_m, shift_f, scale_f, gate_f = (mod_lat[:, i] for i in range(N_MOD))
        mod_ctx = mod[bsz].reshape(N_MOD, 1, 1, d)

        w_all = w_in[l].astype(BF16)
        w_ba = _pad_cols(w_in[l][:, n_main:], LANES).astype(BF16)
        g_pre = g_pre_mix[l][None, :]
        p_lat, ba_lat = _inproj(_prenorm(x, g_pre, shift_m, scale_m), w_all, n_main, w_ba)
        h_ctx = _prenorm(ctx, g_pre, mod_ctx[0], mod_ctx[1]).reshape(1, bsz * tctx, d)
        p_ctx, ba_ctx = _inproj(h_ctx, w_all, n_main, w_ba)
        p_ctx = p_ctx.reshape(bsz, tctx, n_main)
        ba_ctx = ba_ctx.reshape(bsz, tctx, LANES)

        g_lru = _lru(p_lat, p_ctx, lru_conv_w[l], lru_conv_b[l][None, :], lru_w_a[l].astype(BF16),
                     lru_b_a[l], lru_w_x[l].astype(BF16), lru_b_x[l], lru_lambda[l], lw)

        qkv_lat = _gdnpre(p_lat, 2 * lw, gdn_conv_w[l], gh, GRID_W)
        qkv_ctx = _gdnpre(p_ctx, 2 * lw, gdn_conv_w[l], gh, 1)
        prm = jnp.zeros((SUBLANES, LANES), F32)
        prm = prm.at[0, 2 * gh:4 * gh].set(gdn_a_log[l].reshape(-1))
        prm = prm.at[1, 2 * gh:4 * gh].set(gdn_dt_bias[l].reshape(-1))
        ba_vis = _to_visit(ba_lat, GRID_W)
        s_f = _gdn_call("ctx", False, 0, gh, qkv_ctx, ba_ctx, prm, tctx)
        s_b = _gdn_call("ctx", True, 1, gh, qkv_ctx, ba_ctx, prm, tctx)
        o_f, wg_b, wu_b = _gdn_call("fwd", False, 0, gh, qkv_lat, ba_vis, prm, rows, s0=s_f,
                                    casts=(w_ffn_gate[l], w_ffn_up[l]))
        o_fb, wd_b, wo_b = _gdn_call("bwd", True, 1, gh, qkv_lat, ba_vis, prm, rows, s0=s_b, o_f=o_f,
                                     casts=(w_ffn_down[l], w_out[l]))
        y_gdn = _gdnpost(o_fb, p_lat, 2 * lw + 3 * gw, gdn_norm_g[l][None, :], gh, GRID_W)
        y_gdn = y_gdn.reshape(bsz * seq, gw)

        mix = _outproj(g_lru.reshape(bsz * seq, lw), y_gdn, lru_norm_g[l][None, :], wo_b)
        mix = mix.reshape(bsz, seq, d)
        gpm = g_post_mix[l][None, :]
        h_ffn = _post(x, mix, gpm, gate_m, (g_pre_ffn[l][None, :], shift_f, scale_f), final=False)
        y_ffn = _ffn(h_ffn.reshape(bsz * seq, d), wg_b, wu_b, wd_b)
        x = _post(x, mix, gpm, gate_m, (y_ffn.reshape(bsz, seq, d), g_post_ffn[l][None, :], gate_f),
                  final=True)
    return x
```
